```python
import math
import jax, jax.numpy as jnp
from jax import lax
import numpy as np

D_MODEL = 1024
BATCH = 32
SEQ = 256
DEPTH = 1
DEC_BATCH = 8
DEC_SEQ = 2048
PAST_LEN = 512

GRID_W = 64
ATTN_WIDTH = D_MODEL // 2
SSM_WIDTH = D_MODEL - ATTN_WIDTH
HEAD_DIM = 64
N_HEADS = ATTN_WIDTH // HEAD_DIM
N_KV_HEADS = 2
KV_WIDTH = N_KV_HEADS * HEAD_DIM
SSM_GROUP = 16
N_SSM_GROUPS = SSM_WIDTH // SSM_GROUP
SSM_STATE = 64
IN_WIDTH = ATTN_WIDTH + 2 * KV_WIDTH + SSM_WIDTH
N_EXPERTS = 16
EXPERT_FF = D_MODEL
CAPACITY_FACTOR = 2
Q_BLOCK = 128
ROPE_THETA = 10000.0
ROPE_PAIRS_PER_AXIS = HEAD_DIM // 4
NORM_EPS = 1e-6
N_MOD = 6

kernel_name = 'hybrid_s5_gqa_ec_moe_diffusion_step'


def rmsnorm(x, g):
    xf = x.astype(jnp.float32)
    y = xf * lax.rsqrt(jnp.mean(xf * xf, axis=-1, keepdims=True) + NORM_EPS)
    return (y * g.astype(jnp.float32)).astype(x.dtype)


def axial_rope_tables(n_tokens):
    rows = n_tokens // GRID_W
    row = jnp.repeat(jnp.arange(rows, dtype=jnp.float32), GRID_W)
    col = jnp.tile(jnp.arange(GRID_W, dtype=jnp.float32), rows)
    inv_freq = ROPE_THETA ** (-jnp.arange(ROPE_PAIRS_PER_AXIS, dtype=jnp.float32) / ROPE_PAIRS_PER_AXIS)
    ang = jnp.concatenate([row[:, None] * inv_freq, col[:, None] * inv_freq], axis=-1)
    return jnp.cos(ang), jnp.sin(ang)


def apply_rope(x, cos, sin):
    xf = x.astype(jnp.float32)
    half = HEAD_DIM // 2
    x1, x2 = xf[..., :half], xf[..., half:]
    c = cos[None, :, None, :]
    s = sin[None, :, None, :]
    return jnp.concatenate([x1 * c - x2 * s, x2 * c + x1 * s], axis=-1).astype(x.dtype)


def block_attention(q, k, v):
    b, n_q = q.shape[:2]
    rep = N_HEADS // N_KV_HEADS
    n_blocks = n_q // Q_BLOCK
    qb = q.reshape(b, n_blocks, Q_BLOCK, N_KV_HEADS, rep, HEAD_DIM).transpose(1, 0, 2, 3, 4, 5)
    scale = HEAD_DIM ** -0.5

    def one_block(qblk):
        s = jnp.einsum('bqgrd,bsgd->bgrqs', qblk, k).astype(jnp.float32) * scale
        p = jax.nn.softmax(s, axis=-1).astype(v.dtype)
        return jnp.einsum('bgrqs,bsgd->bqgrd', p, v)

    o = lax.map(one_block, qb)
    return o.transpose(1, 0, 2, 3, 4, 5).reshape(b, n_q, ATTN_WIDTH)


def zoh_discretise(a_re, a_im, log_dt, b_re, b_im):
    a_re = a_re.astype(jnp.float32)
    a_im = a_im.astype(jnp.float32)
    b_re = b_re.astype(jnp.float32)
    b_im = b_im.astype(jnp.float32)
    dt = jnp.exp(log_dt.astype(jnp.float32))[:, None]
    mag = jnp.exp(a_re * dt)
    abar_re = mag * jnp.cos(a_im * dt)
    abar_im = mag * jnp.sin(a_im * dt)
    den = a_re * a_re + a_im * a_im
    n_re = abar_re - 1.0
    n_im = abar_im
    coef_re = (n_re * a_re + n_im * a_im) / den
    coef_im = (n_im * a_re - n_re * a_im) / den
    bbar_re = coef_re[..., None] * b_re - coef_im[..., None] * b_im
    bbar_im = coef_re[..., None] * b_im + coef_im[..., None] * b_re
    return abar_re, abar_im, bbar_re, bbar_im


def complex_affine_combine(earlier, later):
    a1r, a1i, b1r, b1i = earlier
    a2r, a2i, b2r, b2i = later
    ar = a2r * a1r - a2i * a1i
    ai = a2r * a1i + a2i * a1r
    br = a2r * b1r - a2i * b1i + b2r
    bi = a2r * b1i + a2i * b1r + b2i
    return ar, ai, br, bi


def s5_direction_scan(u_grp, abar_re, abar_im, bbar_re, bbar_im, h0, reverse):
    bu_re = jnp.einsum('blgi,gpi->blgp', u_grp, bbar_re)
    bu_im = jnp.einsum('blgi,gpi->blgp', u_grp, bbar_im)
    a_re = jnp.broadcast_to(abar_re, bu_re.shape)
    a_im = jnp.broadcast_to(abar_im, bu_re.shape)
    cum_re, cum_im, h_re, h_im = lax.associative_scan(
        complex_affine_combine, (a_re, a_im, bu_re, bu_im), reverse=reverse, axis=1)
    if h0 is not None:
        h0_re, h0_im = h0[0][:, None], h0[1][:, None]
        h_re = h_re + cum_re * h0_re - cum_im * h0_im
        h_im = h_im + cum_re * h0_im + cum_im * h0_re
    return h_re, h_im


def s5_mixer(u, a_re, a_im, log_dt, b_re, b_im, c_re, c_im, d_skip, w_glu, b_glu, h0):
    b, n = u.shape[:2]
    uf = u.astype(jnp.float32)
    u_grp = uf.reshape(b, n, N_SSM_GROUPS, SSM_GROUP)
    y = uf * d_skip.astype(jnp.float32)
    finals = []
    for direction in range(2):
        abr, abi, bbr, bbi = zoh_discretise(a_re[direction], a_im[direction], log_dt[direction],
                                            b_re[direction], b_im[direction])
        h0_d = None if h0 is None else (h0[:, direction, 0].astype(jnp.float32),
                                        h0[:, direction, 1].astype(jnp.float32))
        h_re, h_im = s5_direction_scan(u_grp, abr, abi, bbr, bbi, h0_d, reverse=(direction == 1))
        y_d = (jnp.einsum('blgp,gip->blgi', h_re, c_re[direction].astype(jnp.float32))
               - jnp.einsum('blgp,gip->blgi', h_im, c_im[direction].astype(jnp.float32)))
        y = y + y_d.reshape(b, n, SSM_WIDTH)
        last = n - 1 if direction == 0 else 0
        finals.append(jnp.stack([h_re[:, last], h_im[:, last]], axis=1))
    y = jax.nn.gelu(y)
    y = y * jax.nn.sigmoid(y @ w_glu.astype(jnp.float32) + b_glu.astype(jnp.float32))
    return y.astype(u.dtype), jnp.stack(finals, axis=1)


def expert_choice_ffn(h, w_router, w_gate, w_up, w_down):
    b, n, d = h.shape
    cap = CAPACITY_FACTOR * n // N_EXPERTS
    logits = jnp.einsum('bld,de->ble', h, w_router).astype(jnp.float32)
    aff = jax.nn.softmax(logits, axis=-1)
    gate, idx = lax.top_k(aff.transpose(0, 2, 1), cap)
    x_sel = jax.vmap(lambda hb, ib: hb[ib])(h, idx)
    hid = jax.nn.silu(jnp.einsum('becd,edf->becf', x_sel, w_gate)) * jnp.einsum('becd,edf->becf', x_sel, w_up)
    y = jnp.einsum('becf,efd->becd', hid, w_down) * gate[..., None].astype(h.dtype)
    return jax.vmap(lambda ib, yb: jnp.zeros((n, d), h.dtype).at[ib.reshape(-1)].add(yb.reshape(-1, d)))(idx, y)


def adaln_modulation(cond, w_mod, b_mod):
    m = jax.nn.silu(cond) @ w_mod + b_mod
    return jnp.split(m[:, None, :], N_MOD, axis=-1)


def trunk_layer(x, cond, p, rope, ctx_k, ctx_v, ctx_state):
    shift1, scale1, gate1, shift2, scale2, gate2 = adaln_modulation(cond, p['w_mod'], p['b_mod'])
    b, n = x.shape[:2]
    h = rmsnorm(x, p['norm1_g']) * (1 + scale1) + shift1
    z = h @ p['w_in']
    q, k, v, u = jnp.split(z, [ATTN_WIDTH, ATTN_WIDTH + KV_WIDTH, ATTN_WIDTH + 2 * KV_WIDTH], axis=-1)
    q = rmsnorm(q.reshape(b, n, N_HEADS, HEAD_DIM), p['q_norm_g'])
    k = rmsnorm(k.reshape(b, n, N_KV_HEADS, HEAD_DIM), p['k_norm_g'])
    v = v.reshape(b, n, N_KV_HEADS, HEAD_DIM)
    if rope is None:
        k_all, v_all = k, v
    else:
        q = apply_rope(q, rope[0], rope[1])
        k = apply_rope(k, rope[0], rope[1])
        k_all = jnp.concatenate([k, ctx_k], axis=1)
        v_all = jnp.concatenate([v, ctx_v], axis=1)
    attn_out = block_attention(q, k_all, v_all)
    ssm_out, ssm_state = s5_mixer(u, p['ssm_a_re'], p['ssm_a_im'], p['ssm_log_dt'], p['ssm_b_re'], p['ssm_b_im'],
                                  p['ssm_c_re'], p['ssm_c_im'], p['ssm_d'], p['w_glu'], p['b_glu'], ctx_state)
    x = x + gate1 * (jnp.concatenate([attn_out, ssm_out], axis=-1) @ p['w_out'])
    h2 = rmsnorm(x, p['norm2_g']) * (1 + scale2) + shift2
    x = x + gate2 * expert_choice_ffn(h2, p['w_router'], p['w_gate'], p['w_up'], p['w_down'])
    return x, k, v, ssm_state


def setup_inputs(seed: int = 0) -> dict:
    key = jax.random.key(seed)
    ks = jax.random.split(key, 32)
    f32 = jnp.float32

    def nrm(k, shape, scale):
        return jax.random.normal(k, shape, f32) * scale

    G, P, I = N_SSM_GROUPS, SSM_STATE, SSM_GROUP
    return {
        'x_prompt': nrm(ks[0], (BATCH, SEQ, D_MODEL), 1.0),
        'x_sample': nrm(ks[1], (DEC_BATCH, DEC_SEQ, D_MODEL), 1.0),
        'cache_k': nrm(ks[2], (DEC_BATCH, DEPTH, PAST_LEN, N_KV_HEADS, HEAD_DIM), 1.0),
        'cache_v': nrm(ks[3], (DEC_BATCH, DEPTH, PAST_LEN, N_KV_HEADS, HEAD_DIM), 1.0),
        'state_ssm': nrm(ks[4], (DEC_BATCH, DEPTH, 2, 2, G, P), 0.3),
        'c': nrm(ks[5], (DEC_BATCH, D_MODEL), 1.0),
        'c_ctx': nrm(ks[6], (D_MODEL,), 1.0),
        'norm1_g': 1.0 + nrm(ks[7], (DEPTH, D_MODEL), 0.05),
        'norm2_g': 1.0 + nrm(ks[8], (DEPTH, D_MODEL), 0.05),
        'w_mod': nrm(ks[9], (DEPTH, D_MODEL, N_MOD * D_MODEL), D_MODEL ** -0.5),
        'b_mod': nrm(ks[10], (DEPTH, N_MOD * D_MODEL), 0.01),
        'w_in': nrm(ks[11], (DEPTH, D_MODEL, IN_WIDTH), D_MODEL ** -0.5),
        'q_norm_g': 1.0 + nrm(ks[12], (DEPTH, HEAD_DIM), 0.05),
        'k_norm_g': 1.0 + nrm(ks[13], (DEPTH, HEAD_DIM), 0.05),
        'ssm_a_re': -0.5 + nrm(ks[14], (DEPTH, 2, G, P), 0.01),
        'ssm_a_im': math.pi * jnp.arange(P, dtype=f32) + nrm(ks[15], (DEPTH, 2, G, P), 0.01),
        'ssm_log_dt': jax.random.uniform(ks[16], (DEPTH, 2, G), f32, math.log(1e-3), math.log(1e-1)),
        'ssm_b_re': nrm(ks[17], (DEPTH, 2, G, P, I), (2 * I) ** -0.5),
        'ssm_b_im': nrm(ks[18], (DEPTH, 2, G, P, I), (2 * I) ** -0.5),
        'ssm_c_re': nrm(ks[19], (DEPTH, 2, G, I, P), P ** -0.5),
        'ssm_c_im': nrm(ks[20], (DEPTH, 2, G, I, P), P ** -0.5),
        'ssm_d': nrm(ks[21], (DEPTH, SSM_WIDTH), 1.0),
        'w_glu': nrm(ks[22], (DEPTH, SSM_WIDTH, SSM_WIDTH), SSM_WIDTH ** -0.5),
        'b_glu': nrm(ks[23], (DEPTH, SSM_WIDTH), 0.01),
        'w_out': nrm(ks[24], (DEPTH, D_MODEL, D_MODEL), D_MODEL ** -0.5),
        'w_router': nrm(ks[25], (DEPTH, D_MODEL, N_EXPERTS), D_MODEL ** -0.5),
        'w_gate': nrm(ks[26], (DEPTH, N_EXPERTS, D_MODEL, EXPERT_FF), D_MODEL ** -0.5),
        'w_up': nrm(ks[27], (DEPTH, N_EXPERTS, D_MODEL, EXPERT_FF), D_MODEL ** -0.5),
        'w_down': nrm(ks[28], (DEPTH, N_EXPERTS, EXPERT_FF, D_MODEL), EXPERT_FF ** -0.5),
        'final_norm_g': 1.0 + nrm(ks[29], (D_MODEL,), 0.05),
    }


def reference(x_prompt, x_sample, cache_k, cache_v, state_ssm, c, c_ctx,
              norm1_g, norm2_g, w_mod, b_mod, w_in, q_norm_g, k_norm_g,
              ssm_a_re, ssm_a_im, ssm_log_dt, ssm_b_re, ssm_b_im, ssm_c_re, ssm_c_im, ssm_d,
              w_glu, b_glu, w_out, w_router, w_gate, w_up, w_down, final_norm_g):
    rope_lat = axial_rope_tables(x_sample.shape[1])
    x_p, x_s = x_prompt, x_sample
    new_k, new_v, new_s = [], [], []
    for l in range(DEPTH):
        p = {
            'norm1_g': norm1_g[l], 'norm2_g': norm2_g[l], 'w_mod': w_mod[l], 'b_mod': b_mod[l],
            'w_in': w_in[l], 'q_norm_g': q_norm_g[l], 'k_norm_g': k_norm_g[l],
            'ssm_a_re': ssm_a_re[l], 'ssm_a_im': ssm_a_im[l], 'ssm_log_dt': ssm_log_dt[l],
            'ssm_b_re': ssm_b_re[l], 'ssm_b_im': ssm_b_im[l], 'ssm_c_re': ssm_c_re[l], 'ssm_c_im': ssm_c_im[l],
            'ssm_d': ssm_d[l], 'w_glu': w_glu[l], 'b_glu': b_glu[l], 'w_out': w_out[l],
            'w_router': w_router[l], 'w_gate': w_gate[l], 'w_up': w_up[l], 'w_down': w_down[l],
        }
        x_p, k_ctx, v_ctx, st_ctx = trunk_layer(x_p, c_ctx[None, :], p, None, None, None, None)
        new_k.append(k_ctx)
        new_v.append(v_ctx)
        new_s.append(st_ctx.astype(x_p.dtype))
        x_s, _, _, _ = trunk_layer(x_s, c, p, rope_lat, cache_k[:, l], cache_v[:, l], state_ssm[:, l])
    y_prompt = rmsnorm(x_p, final_norm_g)
    y_sample = rmsnorm(x_s, final_norm_g)
    new_cache_k = jnp.stack(new_k, axis=1)
    new_cache_v = jnp.stack(new_v, axis=1)
    new_state_ssm = jnp.stack(new_s, axis=1)
    return (y_prompt, y_sample, new_cache_k, new_cache_v, new_state_ssm)
```

```python
import functools

import jax
import jax.numpy as jnp
from jax import lax
from jax.experimental import pallas as pl
from jax.experimental.pallas import tpu as pltpu

F32 = jnp.float32
BF16 = jnp.bfloat16

HEAD_DIM = 64
N_HEADS = 8
N_KV_HEADS = 2
REP = N_HEADS // N_KV_HEADS
ATTN_WIDTH = N_HEADS * HEAD_DIM
KV_WIDTH = N_KV_HEADS * HEAD_DIM
SSM_GROUP = 16
SSM_STATE = 64
N_EXPERTS = 16
CAPACITY_FACTOR = 2
GRID_W = 64
ROPE_THETA = 10000.0
NORM_EPS = 1e-6
N_MOD = 6
CHUNK = 16
LANES = 128
VMEM_LIMIT = 56 * 1024 * 1024


def _params(n_grid_axes, vmem=VMEM_LIMIT):
    return pltpu.CompilerParams(dimension_semantics=("arbitrary",) * n_grid_axes,
                                vmem_limit_bytes=vmem)


def _chunked_spec(width, tm):
    return pl.BlockSpec((width // LANES, tm // CHUNK, None, CHUNK, LANES), lambda i, j: (0, j, i, 0, 0))


def _dot(a, b):
    return jnp.dot(a, b, preferred_element_type=F32)


def _dot_nt(a, b):
    return lax.dot_general(a, b, (((1,), (1,)), ((), ())), preferred_element_type=F32)


def _rms_rows(x, g):
    return x * lax.rsqrt(jnp.mean(x * x, axis=-1, keepdims=True) + NORM_EPS) * g


def _mod_kernel(cond_ref, w_ref, b_ref, out_ref):
    c = cond_ref[...]
    s = (c * jax.nn.sigmoid(c)).astype(BF16)
    out_ref[...] = _dot(s, w_ref[...].astype(BF16)) + b_ref[...]


def _modulation(cond, w_mod, b_mod):
    rows, d = cond.shape
    n = w_mod.shape[1]
    tn = 1024
    return pl.pallas_call(
        _mod_kernel,
        grid=(n // tn,),
        in_specs=[pl.BlockSpec((rows, d), lambda j: (0, 0)),
                  pl.BlockSpec((d, tn), lambda j: (0, j)),
                  pl.BlockSpec((1, tn), lambda j: (0, j))],
        out_specs=pl.BlockSpec((rows, tn), lambda j: (0, j)),
        out_shape=jax.ShapeDtypeStruct((rows, n), F32),
        compiler_params=_params(1),
        name="mod",
    )(cond, w_mod, b_mod.reshape(1, n))


def _head_sumsq(z, bd):
    z2 = z * z
    hi = z2.astype(BF16)
    lo = (z2 - hi.astype(F32)).astype(BF16)
    return _dot(hi, bd) + _dot(lo, bd)


def _inproj_kernel(*refs, rope):
    if rope:
        (x_ref, mod_ref, g1_ref, win_ref, qg_ref, kg_ref, bd_ref, cos_ref, sin_ref,
         q_out, k_out, v_out, u_out) = refs
    else:
        (x_ref, mod_ref, g1_ref, win_ref, qg_ref, kg_ref, bd_ref,
         q_out, k_out, v_out, u_out) = refs
    x = x_ref[...]
    m = mod_ref[...]
    shift1, scale1 = m[0:1], m[1:2]
    h = _rms_rows(x, g1_ref[...]) * (1.0 + scale1) + shift1
    z = _dot(h.astype(BF16), win_ref[...])
    bd = bd_ref[...]
    if rope:
        cos = cos_ref[...]
        sin = sin_ref[...]
        lane = lax.broadcasted_iota(jnp.int32, cos.shape, 1)
        first_half = (lane % HEAD_DIM) < (HEAD_DIM // 2)

    def norm_rope(zc, g):
        zn = zc * lax.rsqrt(_head_sumsq(zc, bd) * (1.0 / HEAD_DIM) + NORM_EPS) * g
        if rope:
            rot = jnp.where(first_half,
                            -pltpu.roll(zn, LANES - HEAD_DIM // 2, 1),
                            pltpu.roll(zn, HEAD_DIM // 2, 1))
            zn = zn * cos + rot * sin
        return zn

    for i in range(ATTN_WIDTH // LANES):
        qc = norm_rope(z[:, i * LANES:(i + 1) * LANES], qg_ref[...])
        q_out[:, i * LANES:(i + 1) * LANES] = (qc * (HEAD_DIM ** -0.5)).astype(BF16)
    k_out[...] = norm_rope(z[:, ATTN_WIDTH:ATTN_WIDTH + KV_WIDTH], kg_ref[...])
    v_out[...] = z[:, ATTN_WIDTH + KV_WIDTH:ATTN_WIDTH + 2 * KV_WIDTH]
    u0 = ATTN_WIDTH + 2 * KV_WIDTH
    for gb in range(u_out.shape[0]):
        u_out[gb] = z[:, u0 + gb * LANES:u0 + (gb + 1) * LANES].reshape(u_out.shape[1:])


def _inproj(x, mod, g1, w_in_bf, qg, kg, bd, rope_tabs):
    b, n, d = x.shape
    tm = min(n, 512)
    ssm_w = w_in_bf.shape[1] - ATTN_WIDTH - 2 * KV_WIDTH
    shared = mod.shape[0] == 1
    mod_idx = (lambda i, j: (0, 0, 0)) if shared else (lambda i, j: (i, 0, 0))
    const2 = lambda i, j: (0, 0)
    in_specs = [pl.BlockSpec((None, tm, d), lambda i, j: (i, j, 0)),
                pl.BlockSpec((None, N_MOD, d), mod_idx),
                pl.BlockSpec((1, d), const2),
                pl.BlockSpec(w_in_bf.shape, const2),
                pl.BlockSpec((1, LANES), const2),
                pl.BlockSpec((1, LANES), const2),
                pl.BlockSpec((LANES, LANES), const2)]
    args = [x, mod, g1, w_in_bf, qg, kg, bd]
    if rope_tabs is not None:
        in_specs += [pl.BlockSpec((tm, LANES), lambda i, j: (j, 0))] * 2
        args += list(rope_tabs)
    tok = lambda w: pl.BlockSpec((None, tm, w), lambda i, j: (i, j, 0))
    return pl.pallas_call(
        functools.partial(_inproj_kernel, rope=rope_tabs is not None),
        grid=(b, n // tm),
        in_specs=in_specs,
        out_specs=[tok(ATTN_WIDTH), tok(KV_WIDTH), tok(KV_WIDTH), _chunked_spec(ssm_w, tm)],
        out_shape=[jax.ShapeDtypeStruct((b, n, ATTN_WIDTH), BF16),
                   jax.ShapeDtypeStruct((b, n, KV_WIDTH), F32),
                   jax.ShapeDtypeStruct((b, n, KV_WIDTH), F32),
                   jax.ShapeDtypeStruct((ssm_w // LANES, n // CHUNK, b, CHUNK, LANES), F32)],
        compiler_params=_params(2),
        name="inproj",
    )(*args)


KV_FILL_ROWS = 256


def _attn_kernel(*refs, n_src):
    q_ref = refs[0]
    kv_refs = refs[1:1 + 2 * n_src]
    out_ref, k4_scr, v4_scr = refs[1 + 2 * n_src:]
    gw = REP * HEAD_DIM

    @pl.when(pl.program_id(1) == 0)
    def _():
        lane = lax.broadcasted_iota(jnp.int32, (KV_FILL_ROWS, KV_WIDTH), 1)
        first = lane < HEAD_DIM
        for dst, srcs in ((k4_scr, kv_refs[0::2]), (v4_scr, kv_refs[1::2])):
            row0 = 0
            for src in srcs:
                for r in range(0, src.shape[0], KV_FILL_ROWS):
                    x = src[r:r + KV_FILL_ROWS, :]
                    sw = pltpu.roll(x, HEAD_DIM, 1)
                    for g, xx in enumerate((jnp.where(first, x, sw), jnp.where(first, sw, x))):
                        xx = xx.astype(BF16)
                        dst[g, row0 + r:row0 + r + KV_FILL_ROWS, :] = jnp.concatenate([xx, xx], axis=1)
                row0 += src.shape[0]

    lane_head = lax.broadcasted_iota(jnp.int32, (q_ref.shape[0], gw), 1) // HEAD_DIM
    for g in range(N_KV_HEADS):
        qg = q_ref[:, g * gw:(g + 1) * gw]
        acc = jnp.zeros((q_ref.shape[0], gw), F32)
        for r in range(REP):
            qm = jnp.where(lane_head == r, qg, jnp.zeros_like(qg))
            s = _dot_nt(qm, k4_scr[g])
            p = jnp.exp(s - jnp.max(s, axis=-1, keepdims=True))
            l = jnp.sum(p, axis=-1, keepdims=True)
            o = _dot(p.astype(BF16), v4_scr[g])
            acc = jnp.where(lane_head == r, o / l, acc)
        out_ref[:, g * gw:(g + 1) * gw] = acc.astype(BF16)


def _attention(q, kv_sources):
    b, n, _ = q.shape
    s = sum(k.shape[1] for k, _ in kv_sources)
    tq = min(n, 256)
    gw = REP * HEAD_DIM
    in_specs = [pl.BlockSpec((None, tq, ATTN_WIDTH), lambda i, j: (i, j, 0))]
    args = [q]
    for k, v in kv_sources:
        in_specs += [pl.BlockSpec((None, k.shape[1], KV_WIDTH), lambda i, j: (i, 0, 0))] * 2
        args += [k, v]
    return pl.pallas_call(
        functools.partial(_attn_kernel, n_src=len(kv_sources)),
        grid=(b, n // tq),
        in_specs=in_specs,
        out_specs=pl.BlockSpec((None, tq, ATTN_WIDTH), lambda i, j: (i, j, 0)),
        out_shape=jax.ShapeDtypeStruct((b, n, ATTN_WIDTH), BF16),
        scratch_shapes=[pltpu.VMEM((N_KV_HEADS, s, gw), BF16), pltpu.VMEM((N_KV_HEADS, s, gw), BF16)],
        compiler_params=_params(2),
        name="attn",
    )(*args)


def _cmul(ar, ai, br, bi):
    return ar * br - ai * bi, ar * bi + ai * br


def _cpow(ar, ai, e, shape):
    rr = jnp.ones(shape, F32)
    ri = jnp.zeros(shape, F32)
    pr = jnp.broadcast_to(ar, shape)
    pi = jnp.broadcast_to(ai, shape)
    e = jnp.broadcast_to(e, shape)
    for k in range(CHUNK.bit_length()):
        nr, ni = _cmul(rr, ri, pr, pi)
        bit = ((e >> k) & 1) == 1
        rr = jnp.where(bit, nr, rr)
        ri = jnp.where(bit, ni, ri)
        pr, pi = _cmul(pr, pi, pr, pi)
    return rr, ri


def _zoh(ar, ai, dt):
    mag = jnp.exp(ar * dt)
    abr = mag * jnp.cos(ai * dt)
    abi = mag * jnp.sin(ai * dt)
    den = ar * ar + ai * ai
    nre = abr - 1.0
    nim = abi
    cre = (nre * ar + nim * ai) / den
    cim = (nim * ar - nre * ai) / den
    return abr, abi, cre, cim


def _s5prep_kernel(arc_ref, aic_ref, arr_ref, air_ref, ldt_ref, btr_ref, bti_ref, ctr_ref, cti_ref,
                   m_out, pre_out, pim_out, qre_out, qim_out, a16r_out, a16i_out):
    P, I, T = SSM_STATE, SSM_GROUP, CHUNK
    W = T * I
    hi = lax.Precision.HIGHEST
    kalls = []
    for d in range(2):
        dt = jnp.exp(ldt_ref[d])
        abr_c, abi_c, _, _ = _zoh(arc_ref[d], aic_ref[d], dt)
        abr_r, abi_r, cre_r, cim_r = _zoh(arr_ref[d], air_ref[d], dt)
        bbr, bbi = _cmul(cre_r, cim_r, btr_ref[d], bti_ref[d])

        blk = lax.broadcasted_iota(jnp.int32, (1, W), 1) // I
        e_w = blk if d == 0 else (T - 1) - blk
        pwr, pwi = _cpow(abr_c, abi_c, e_w, (P, W))
        ctr, cti = ctr_ref[d], cti_ref[d]
        w_re, w_im = _cmul(pwr, pwi, ctr, cti)
        kall = (jnp.dot(bbr, w_re, precision=hi, preferred_element_type=F32)
                - jnp.dot(bbi, w_im, precision=hi, preferred_element_type=F32))
        kalls.append(kall)

        srow = lax.broadcasted_iota(jnp.int32, (W, 1), 0) // I
        e_p = (T - 1) - srow if d == 0 else srow
        ppr, ppi = _cpow(abr_r, abi_r, e_p, (W, P))
        bt_r = jnp.concatenate([bbr] * T, axis=0)
        bt_i = jnp.concatenate([bbi] * T, axis=0)
        p_re, p_im = _cmul(ppr, ppi, bt_r, bt_i)
        pre_out[d] = p_re.astype(BF16)
        pim_out[d] = p_im.astype(BF16)

        q_re, q_im = _cmul(w_re, w_im, jnp.broadcast_to(abr_c, (P, W)), jnp.broadcast_to(abi_c, (P, W)))
        qre_out[d] = q_re.astype(BF16)
        qim_out[d] = (-q_im).astype(BF16)

        a16r, a16i = _cpow(abr_r, abi_r, jnp.full((1, P), T, jnp.int32), (1, P))
        a16r_out[d] = a16r
        a16i_out[d] = a16i

    lane = lax.broadcasted_iota(jnp.int32, (I, W), 1)
    for s in range(T):
        fwd = jnp.where(lane >= I * s, pltpu.roll(kalls[0], I * s, 1) if s else kalls[0], 0.0)
        sh = (I * (s + 1)) % W
        bwd = jnp.where(lane < I * (s + 1), pltpu.roll(kalls[1], sh, 1) if sh else kalls[1], 0.0)
        m_out[s * I:(s + 1) * I, :] = (fwd + bwd).astype(BF16)


def _s5_prep(a_re, a_im, log_dt, b_re, b_im, c_re, c_im):
    _, G, P = a_re.shape
    I, T = SSM_GROUP, CHUNK
    W = T * I
    col = lambda a: a.reshape(2, G, P, 1)
    row = lambda a: a.reshape(2, G, 1, P)
    bt = lambda b: jnp.swapaxes(b, 2, 3)
    ct = lambda c: jnp.tile(jnp.swapaxes(c, 2, 3), (1, 1, 1, T))
    spec = lambda r, c: pl.BlockSpec((2, None, r, c), lambda g: (0, g, 0, 0))
    shp = lambda r, c, dt: jax.ShapeDtypeStruct((2, G, r, c), dt)
    return pl.pallas_call(
        _s5prep_kernel,
        grid=(G,),
        in_specs=[spec(P, 1), spec(P, 1), spec(1, P), spec(1, P), spec(1, 1),
                  spec(I, P), spec(I, P), spec(P, W), spec(P, W)],
        out_specs=[pl.BlockSpec((None, W, W), lambda g: (g, 0, 0)),
                   spec(W, P), spec(W, P), spec(P, W), spec(P, W), spec(1, P), spec(1, P)],
        out_shape=[jax.ShapeDtypeStruct((G, W, W), BF16),
                   shp(W, P, BF16), shp(W, P, BF16), shp(P, W, BF16), shp(P, W, BF16),
                   shp(1, P, F32), shp(1, P, F32)],
        compiler_params=_params(1),
        name="s5prep",
    )(col(a_re), col(a_im), row(a_re), row(a_im), log_dt.reshape(2, G, 1, 1),
      bt(b_re), bt(b_im), ct(c_re), ct(c_im))


RELAY_ROWS = 32


def _s5_kernel(*refs, batch, has_h0):
    if has_h0:
        (u_ref, m_ref, pre_ref, pim_ref, qre_ref, qim_ref, a16r_ref, a16i_ref, h0_ref,
         y_ref, fin_ref, z_scr, yg_scr, s_scr, h_scr) = refs
    else:
        (u_ref, m_ref, pre_ref, pim_ref, qre_ref, qim_ref, a16r_ref, a16i_ref,
         y_ref, fin_ref, z_scr, yg_scr, s_scr, h_scr) = refs
    I, T, RT = SSM_GROUP, CHUNK, RELAY_ROWS
    gpb = LANES // I
    rows = u_ref.shape[0] // T
    n_chunks = rows // batch
    lane_blk = lax.broadcasted_iota(jnp.int32, (RT, LANES), 1) // I

    for g in range(gpb):
        def to_group(i, carry):
            r0 = pl.multiple_of(i * RT, RT)
            for h in range(T * I // LANES):
                acc = None
                for p in range(gpb):
                    s = p + gpb * h
                    r = u_ref[pl.ds(r0 * T + s, RT, stride=T), :]
                    sh = (I * (p - g)) % LANES
                    if sh:
                        r = pltpu.roll(r, sh, 1)
                    acc = r if acc is None else jnp.where(lane_blk == p, r, acc)
                z_scr[pl.ds(r0, RT), h * LANES:(h + 1) * LANES] = acc.astype(BF16)
            return carry

        lax.fori_loop(0, rows // RT, to_group, 0)
        u = z_scr[...]
        for d in range(2):
            s_scr[d, 0] = _dot(u, pre_ref[d, g])
            s_scr[d, 1] = _dot(u, pim_ref[d, g])

        a = [(jnp.broadcast_to(a16r_ref[d, g], (batch, SSM_STATE)),
              jnp.broadcast_to(a16i_ref[d, g], (batch, SSM_STATE))) for d in range(2)]
        if has_h0:
            init = tuple(h0_ref[g, d, c] for d in range(2) for c in range(2))
        else:
            init = tuple(jnp.zeros((batch, SSM_STATE), F32) for _ in range(4))

        def step(k, carry):
            out = []
            for d in range(2):
                hr, hi = carry[2 * d], carry[2 * d + 1]
                c = k if d == 0 else n_chunks - 1 - k
                rs = pl.ds(pl.multiple_of(c * batch, batch), batch)
                h_scr[d, 0, rs, :] = hr
                h_scr[d, 1, rs, :] = hi
                ar, ai = a[d]
                out.append(ar * hr - ai * hi + s_scr[d, 0, rs, :])
                out.append(ar * hi + ai * hr + s_scr[d, 1, rs, :])
            return tuple(out)

        fin = lax.fori_loop(0, n_chunks, step, init)
        for d in range(2):
            fin_ref[g, d, 0] = fin[2 * d]
            fin_ref[g, d, 1] = fin[2 * d + 1]

        y = _dot(u, m_ref[g])
        for d in range(2):
            y = y + _dot(h_scr[d, 0].astype(BF16), qre_ref[d, g]) + _dot(h_scr[d, 1].astype(BF16), qim_ref[d, g])
        yg_scr[...] = y

        def to_tokens(i, carry):
            r0 = pl.multiple_of(i * RT, RT)
            for t in range(T):
                piece = yg_scr[pl.ds(r0, RT), (t // gpb) * LANES:(t // gpb + 1) * LANES]
                sh = (I * (g - t % gpb)) % LANES
                if sh:
                    piece = pltpu.roll(piece, sh, 1)
                rs = pl.ds(r0 * T + t, RT, stride=T)
                if g:
                    piece = jnp.where(lane_blk == g, piece, y_ref[rs, :])
                y_ref[rs, :] = piece
            return carry

        lax.fori_loop(0, rows // RT, to_tokens, 0)


def _s5_scan(u_blk, ops, h0):
    nb, nc, b, _, _ = u_blk.shape
    gpb = LANES // SSM_GROUP
    W = CHUNK * SSM_GROUP
    P = SSM_STATE
    rows = nc * b
    m, pre, pim, qre, qim, a16r, a16i = ops
    spec = lambda r, c: pl.BlockSpec((2, gpb, r, c), lambda j: (0, j, 0, 0))
    act = pl.BlockSpec((None, rows * CHUNK, LANES), lambda j: (j, 0, 0))
    st = pl.BlockSpec((None, gpb, 2, 2, b, P), lambda j: (j, 0, 0, 0, 0, 0))
    in_specs = [act, pl.BlockSpec((gpb, W, W), lambda j: (j, 0, 0)),
                spec(W, P), spec(W, P), spec(P, W), spec(P, W), spec(1, P), spec(1, P)]
    args = [u_blk.reshape(nb, rows * CHUNK, LANES), m, pre, pim, qre, qim, a16r, a16i]
    if h0 is not None:
        in_specs.append(st)
        args.append(h0.transpose(3, 1, 2, 0, 4).reshape(nb, gpb, 2, 2, b, P))
    y, fin = pl.pallas_call(
        functools.partial(_s5_kernel, batch=b, has_h0=h0 is not None),
        grid=(nb,),
        in_specs=in_specs,
        out_specs=[act, st],
        out_shape=[jax.ShapeDtypeStruct((nb, rows * CHUNK, LANES), F32),
                   jax.ShapeDtypeStruct((nb, gpb, 2, 2, b, P), F32)],
        scratch_shapes=[pltpu.VMEM((rows, W), BF16), pltpu.VMEM((rows, W), F32),
                        pltpu.VMEM((2, 2, rows, P), F32), pltpu.VMEM((2, 2, rows, P), F32)],
        compiler_params=_params(1),
        name="s5",
    )(*args)
    fin = fin.reshape(nb * gpb, 2, 2, b, P).transpose(3, 1, 2, 0, 4)
    return y.reshape(u_blk.shape), fin


def _post_kernel(x_ref, attn_ref, y_ref, u_ref, d_ref, wglu_ref, bglu_ref, wout_ref, mod_ref, g2_ref,
                 wrt_ref, x1_out, h2_out, aff_out):
    m = mod_ref[...]
    gate1, shift2, scale2 = m[2:3], m[3:4], m[4:5]
    tm = x_ref.shape[0]
    tokens = lambda r: jnp.concatenate([r[gb].reshape(tm, LANES) for gb in range(r.shape[0])], axis=1)
    y = tokens(y_ref) + tokens(u_ref) * d_ref[...]
    y = jax.nn.gelu(y)
    y = y * jax.nn.sigmoid(_dot(y.astype(BF16), wglu_ref[...]) + bglu_ref[...])
    aw = attn_ref.shape[1]
    proj = _dot(attn_ref[...], wout_ref[:aw, :]) + _dot(y.astype(BF16), wout_ref[aw:, :])
    x1 = x_ref[...] + gate1 * proj
    x1_out[...] = x1
    h2 = (_rms_rows(x1, g2_ref[...]) * (1.0 + scale2) + shift2).astype(BF16)
    h2_out[...] = h2
    logits = _dot_nt(wrt_ref[...], h2)
    e = jnp.exp(logits - jnp.max(logits, axis=0, keepdims=True))
    aff_out[...] = e / jnp.sum(e, axis=0, keepdims=True)


def _post(x, attn, y_ssm, u, d_skip, wglu_bf, b_glu, wout_bf, mod, g2, wrt_bf):
    b, n, d = x.shape
    tm = min(n, 512)
    sw = u.shape[0] * LANES
    shared = mod.shape[0] == 1
    mod_idx = (lambda i, j: (0, 0, 0)) if shared else (lambda i, j: (i, 0, 0))
    const2 = lambda i, j: (0, 0)
    tok = lambda w: pl.BlockSpec((None, tm, w), lambda i, j: (i, j, 0))
    return pl.pallas_call(
        _post_kernel,
        grid=(b, n // tm),
        in_specs=[tok(d), tok(attn.shape[-1]), _chunked_spec(sw, tm), _chunked_spec(sw, tm),
                  pl.BlockSpec((1, sw), const2),
                  pl.BlockSpec(wglu_bf.shape, const2),
                  pl.BlockSpec((1, sw), const2),
                  pl.BlockSpec(wout_bf.shape, const2),
                  pl.BlockSpec((None, N_MOD, d), mod_idx),
                  pl.BlockSpec((1, d), const2),
                  pl.BlockSpec(wrt_bf.shape, const2)],
        out_specs=[tok(d), tok(d), pl.BlockSpec((None, N_EXPERTS, tm), lambda i, j: (i, 0, j))],
        out_shape=[jax.ShapeDtypeStruct((b, n, d), F32),
                   jax.ShapeDtypeStruct((b, n, d), BF16),
                   jax.ShapeDtypeStruct((b, N_EXPERTS, n), F32)],
        compiler_params=_params(2),
        name="post",
    )(x, attn, y_ssm, u, d_skip, wglu_bf, b_glu, wout_bf, mod, g2, wrt_bf)


def _route_kernel(aff_ref, pos_out, *, cap):
    a = aff_ref[...]
    rows, n = a.shape
    capf = jnp.float32(cap)

    def count(mask):
        return jnp.sum(jnp.where(mask, 1.0, 0.0), axis=1, keepdims=True)

    def search(i, bits):
        cand = bits | (jnp.int32(1) << (30 - i))
        ok = count(a >= lax.bitcast_convert_type(cand, F32)) >= capf
        return jnp.where(ok, cand, bits)

    thr_bits = lax.fori_loop(0, 31, search, jnp.zeros((rows, 1), jnp.int32))
    thr = lax.bitcast_convert_type(thr_bits, F32)
    gt = a > thr
    eq = a == thr
    before = (lax.broadcasted_iota(jnp.int32, (n, n), 0)
              < lax.broadcasted_iota(jnp.int32, (n, n), 1))
    tri = jnp.where(before, 1.0, 0.0).astype(BF16)
    eq_rank = _dot(jnp.where(eq, 1.0, 0.0).astype(BF16), tri)
    sel = gt | (eq & (eq_rank < capf - count(gt)))
    pos = _dot(jnp.where(sel, 1.0, 0.0).astype(BF16), tri)
    pos_out[...] = jnp.where(sel, pos, -1.0).astype(jnp.int32)


def _route(aff_t, cap):
    b, e, n = aff_t.shape
    rows = b * e
    pos = pl.pallas_call(
        functools.partial(_route_kernel, cap=cap),
        grid=(1,),
        in_specs=[pl.BlockSpec((rows, n), lambda i: (0, 0))],
        out_specs=pl.BlockSpec((rows, n), lambda i: (0, 0)),
        out_shape=jax.ShapeDtypeStruct((rows, n), jnp.int32),
        compiler_params=_params(1),
        name="route",
    )(aff_t.reshape(rows, n))
    return pos.reshape(b, e, n)


def _gather_kernel(h_ref, pos_ref, aff_ref, x_out, gate_out, p_scr, *, cap, group):
    n = h_ref.shape[0]
    slot = lax.broadcasted_iota(jnp.int32, (cap, n), 0)
    for e0 in range(0, N_EXPERTS, group):
        for j in range(group):
            e = e0 + j
            hit = pos_ref[e:e + 1, :] == slot
            p_scr[j * cap:(j + 1) * cap, :] = jnp.where(hit, 1.0, 0.0).astype(BF16)
            gate_out[e] = jnp.sum(jnp.where(hit, aff_ref[e:e + 1, :], 0.0), axis=1, keepdims=True)
        xs = _dot(p_scr[...], h_ref[...]).astype(BF16)
        for j in range(group):
            x_out[e0 + j] = xs[j * cap:(j + 1) * cap, :]


def _gather(h2, pos, aff_t, cap):
    b, n, d = h2.shape
    group = max(1, min(N_EXPERTS, 512 // cap))
    return pl.pallas_call(
        functools.partial(_gather_kernel, cap=cap, group=group),
        grid=(b,),
        in_specs=[pl.BlockSpec((None, n, d), lambda i: (i, 0, 0)),
                  pl.BlockSpec((None, N_EXPERTS, n), lambda i: (i, 0, 0)),
                  pl.BlockSpec((None, N_EXPERTS, n), lambda i: (i, 0, 0))],
        out_specs=[pl.BlockSpec((None, N_EXPERTS, cap, d), lambda i: (i, 0, 0, 0)),
                   pl.BlockSpec((None, N_EXPERTS, cap, 1), lambda i: (i, 0, 0, 0))],
        out_shape=[jax.ShapeDtypeStruct((b, N_EXPERTS, cap, d), BF16),
                   jax.ShapeDtypeStruct((b, N_EXPERTS, cap, 1), F32)],
        scratch_shapes=[pltpu.VMEM((group * cap, n), BF16)],
        compiler_params=_params(1),
        name="gather",
    )(h2, pos, aff_t)


CAST_ROWS = 128


def _ffn_kernel(x_ref, gate_ref, wg_ref, wu_ref, wd_ref, y_out, wg_bf, wu_bf, wd_bf):
    @pl.when(pl.program_id(1) == 0)
    def _():
        for src, dst in ((wg_ref, wg_bf), (wu_ref, wu_bf), (wd_ref, wd_bf)):
            def cast(i, carry):
                rs = pl.ds(pl.multiple_of(i * CAST_ROWS, CAST_ROWS), CAST_ROWS)
                dst[rs, :] = src[rs, :].astype(BF16)
                return carry
            lax.fori_loop(0, src.shape[0] // CAST_ROWS, cast, 0)

    bt, cap, d = x_ref.shape
    x = x_ref[...].reshape(bt * cap, d)
    g = _dot(x, wg_bf[...])
    hid = (g * jax.nn.sigmoid(g)) * _dot(x, wu_bf[...])
    y = _dot(hid.astype(BF16), wd_bf[...]) * gate_ref[...].reshape(bt * cap, 1)
    y_out[...] = y.astype(BF16).reshape(bt, cap, d)


def _ffn(xsel, gate, wg, wu, wd):
    b, e, cap, d = xsel.shape
    bt = max(1, min(b, 512 // cap))
    f = wg.shape[-1]
    return pl.pallas_call(
        _ffn_kernel,
        grid=(e, b // bt),
        in_specs=[pl.BlockSpec((bt, None, cap, d), lambda i, j: (j, i, 0, 0)),
                  pl.BlockSpec((bt, None, cap, 1), lambda i, j: (j, i, 0, 0)),
                  pl.BlockSpec((None, d, f), lambda i, j: (i, 0, 0)),
                  pl.BlockSpec((None, d, f), lambda i, j: (i, 0, 0)),
                  pl.BlockSpec((None, f, d), lambda i, j: (i, 0, 0))],
        out_specs=pl.BlockSpec((bt, None, cap, d), lambda i, j: (j, i, 0, 0)),
        out_shape=jax.ShapeDtypeStruct((b, e, cap, d), BF16),
        scratch_shapes=[pltpu.VMEM((d, f), BF16), pltpu.VMEM((d, f), BF16), pltpu.VMEM((f, d), BF16)],
        compiler_params=_params(2),
        name="ffn",
    )(xsel, gate, wg, wu, wd)


def _combine_kernel(x1_ref, y_ref, post_ref, mod_ref, gf_ref, out_ref, *, cap):
    tn = x1_ref.shape[0]
    e, _, d = y_ref.shape
    width = min(e * cap, 512)
    per = width // cap
    post = post_ref[...].astype(F32).astype(BF16)
    lane = lax.broadcasted_iota(jnp.int32, (e, width), 1)
    erow = lax.broadcasted_iota(jnp.int32, (e, width), 0)
    slot = (lax.broadcasted_iota(jnp.int32, (tn, width), 1) % cap).astype(F32)
    acc = jnp.zeros((tn, d), F32)
    for c in range(e // per):
        expand = jnp.where(erow == c * per + lane // cap, 1.0, 0.0).astype(BF16)
        hit = _dot(post, expand) == slot
        ys = y_ref[c * per:(c + 1) * per].reshape(width, d)
        acc = acc + _dot(jnp.where(hit, 1.0, 0.0).astype(BF16), ys)
    gate2 = mod_ref[...][5:6]
    out_ref[...] = _rms_rows(x1_ref[...] + gate2 * acc, gf_ref[...])


def _combine(x1, ysel, pos_t, mod, gf, cap):
    b, n, d = x1.shape
    tn = min(n, 512)
    shared = mod.shape[0] == 1
    mod_idx = (lambda i, j: (0, 0, 0)) if shared else (lambda i, j: (i, 0, 0))
    return pl.pallas_call(
        functools.partial(_combine_kernel, cap=cap),
        grid=(b, n // tn),
        in_specs=[pl.BlockSpec((None, tn, d), lambda i, j: (i, j, 0)),
                  pl.BlockSpec((None, N_EXPERTS, cap, d), lambda i, j: (i, 0, 0, 0)),
                  pl.BlockSpec((None, tn, N_EXPERTS), lambda i, j: (i, j, 0)),
                  pl.BlockSpec((None, N_MOD, d), mod_idx),
                  pl.BlockSpec((1, d), lambda i, j: (0, 0))],
        out_specs=pl.BlockSpec((None, tn, d), lambda i, j: (i, j, 0)),
        out_shape=jax.ShapeDtypeStruct((b, n, d), F32),
        compiler_params=_params(2),
        name="combine",
    )(x1, ysel, pos_t, mod, gf)


def _rope_tables(n_tokens):
    pairs = HEAD_DIM // 4
    rows = n_tokens // GRID_W
    row = jnp.repeat(jnp.arange(rows, dtype=F32), GRID_W)
    col = jnp.tile(jnp.arange(GRID_W, dtype=F32), rows)
    inv_freq = ROPE_THETA ** (-jnp.arange(pairs, dtype=F32) / pairs)
    ang = jnp.concatenate([row[:, None] * inv_freq, col[:, None] * inv_freq], axis=-1)
    reps = LANES // (HEAD_DIM // 2)
    return jnp.tile(jnp.cos(ang), (1, reps)), jnp.tile(jnp.sin(ang), (1, reps))


def _trunk(x, mod, w, rope_tabs, ctx_k, ctx_v, ctx_state):
    b, n, d = x.shape
    q, k, v, u = _inproj(x, mod, w["g1"], w["w_in"], w["qg"], w["kg"], w["bd"], rope_tabs)
    kv = [(k, v)]
    if ctx_k is not None:
        kv.append((ctx_k.reshape(b, -1, KV_WIDTH), ctx_v.reshape(b, -1, KV_WIDTH)))
    attn = _attention(q, kv)
    y_ssm, state = _s5_scan(u, w["s5ops"], ctx_state)
    x1, h2, aff_t = _post(x, attn, y_ssm, u, w["d_skip"], w["w_glu"], w["b_glu"], w["w_out"], mod,
                          w["g2"], w["w_router_t"])
    cap = CAPACITY_FACTOR * n // N_EXPERTS
    pos = _route(aff_t, cap)
    xsel, gate = _gather(h2, pos, aff_t, cap)
    ysel = _ffn(xsel, gate, w["w_gate"], w["w_up"], w["w_down"])
    out = _combine(x1, ysel, pos.transpose(0, 2, 1), mod, w["gf"], cap)
    return out, k, v, state


def kernel(x_prompt, x_sample, cache_k, cache_v, state_ssm, c, c_ctx, norm1_g, norm2_g, w_mod, b_mod, w_in, q_norm_g, k_norm_g, ssm_a_re, ssm_a_im, ssm_log_dt, ssm_b_re, ssm_b_im, ssm_c_re, ssm_c_im, ssm_d, w_glu, b_glu, w_out, w_router, w_gate, w_up, w_down, final_norm_g):
    assert norm1_g.shape[0] == 1, "single trunk layer"
    d = x_prompt.shape[-1]
    n_dec = c.shape[0]
    cond = jnp.concatenate([c_ctx[None, :], c, jnp.zeros((16 - 1 - n_dec, d), F32)], axis=0)
    mod = _modulation(cond, w_mod[0], b_mod[0]).reshape(16, N_MOD, d)
    head = jnp.arange(LANES) // HEAD_DIM
    w = {
        "g1": norm1_g, "g2": norm2_g, "gf": final_norm_g.reshape(1, d),
        "w_in": w_in[0].astype(BF16),
        "qg": jnp.tile(q_norm_g, (1, LANES // HEAD_DIM)),
        "kg": jnp.tile(k_norm_g, (1, LANES // HEAD_DIM)),
        "bd": (head[:, None] == head[None, :]).astype(BF16),
        "s5ops": _s5_prep(ssm_a_re[0], ssm_a_im[0], ssm_log_dt[0], ssm_b_re[0], ssm_b_im[0],
                          ssm_c_re[0], ssm_c_im[0]),
        "d_skip": ssm_d, "w_glu": w_glu[0].astype(BF16), "b_glu": b_glu,
        "w_out": w_out[0].astype(BF16),
        "w_router_t": w_router[0].T.astype(BF16),
        "w_gate": w_gate[0], "w_up": w_up[0], "w_down": w_down[0],
    }
    y_p, k_ctx, v_ctx, st_ctx = _trunk(x_prompt, mod[0:1], w, None, None, None, None)
    y_s, _, _, _ = _trunk(x_sample, mod[1:1 + n_dec], w, _rope_tables(x_sample.shape[1]),
                          cache_k[:, 0], cache_v[:, 0], state_ssm[:, 0])
    bp, np_ = x_prompt.shape[:2]
    new_k = k_ctx.reshape(bp, 1, np_, N_KV_HEADS, HEAD_DIM)
    new_v = v_ctx.reshape(bp, 1, np_, N_KV_HEADS, HEAD_DIM)
    return (y_p, y_s, new_k, new_v, st_ctx[:, None])
```

```python
import functools
import math

import jax
import jax.numpy as jnp
from jax import lax
from jax.experimental import pallas as pl
from jax.experimental.pallas import tpu as pltpu

F32 = jnp.float32
BF16 = jnp.bfloat16

HEAD_DIM = 64
N_HEADS = 8
N_KV_HEADS = 2
REP = N_HEADS // N_KV_HEADS
ATTN_WIDTH = N_HEADS * HEAD_DIM
KV_WIDTH = N_KV_HEADS * HEAD_DIM
SSM_GROUP = 16
SSM_STATE = 64
N_EXPERTS = 16
CAPACITY_FACTOR = 2
GRID_W = 64
ROPE_THETA = 10000.0
NORM_EPS = 1e-6
N_MOD = 6
CHUNK = 16
LANES = 128
VMEM_LIMIT = 56 * 1024 * 1024
MOE_ROWS = 1024


def _params(n_grid_axes, vmem=VMEM_LIMIT):
    return pltpu.CompilerParams(dimension_semantics=("arbitrary",) * n_grid_axes,
                                vmem_limit_bytes=vmem)


def _chunked_spec(width, tm):
    return pl.BlockSpec((width // LANES, tm // CHUNK, None, CHUNK, LANES), lambda i, j: (0, j, i, 0, 0))


def _dot(a, b):
    return jnp.dot(a, b, preferred_element_type=F32)


def _dot_nt(a, b):
    return lax.dot_general(a, b, (((1,), (1,)), ((), ())), preferred_element_type=F32)


def _rms_rows(x, g):
    return x * lax.rsqrt(jnp.mean(x * x, axis=-1, keepdims=True) + NORM_EPS) * g


def _mod_kernel(cond_ref, w_ref, b_ref, out_ref):
    c = cond_ref[...]
    s = (c * jax.nn.sigmoid(c)).astype(BF16)
    out_ref[...] = _dot(s, w_ref[...].astype(BF16)) + b_ref[...]


def _modulation(cond, w_mod, b_mod):
    rows, d = cond.shape
    n = w_mod.shape[1]
    tn = 1024
    return pl.pallas_call(
        _mod_kernel,
        grid=(n // tn,),
        in_specs=[pl.BlockSpec((rows, d), lambda j: (0, 0)),
                  pl.BlockSpec((d, tn), lambda j: (0, j)),
                  pl.BlockSpec((1, tn), lambda j: (0, j))],
        out_specs=pl.BlockSpec((rows, tn), lambda j: (0, j)),
        out_shape=jax.ShapeDtypeStruct((rows, n), F32),
        compiler_params=_params(1),
        name="mod",
    )(cond, w_mod, b_mod.reshape(1, n))


def _head_sumsq(z, bd):
    z2 = z * z
    hi = z2.astype(BF16)
    lo = (z2 - hi.astype(F32)).astype(BF16)
    return _dot(hi, bd) + _dot(lo, bd)


def _inproj_kernel(*refs, rope):
    if rope:
        (x_ref, mod_ref, g1_ref, win_ref, qg_ref, kg_ref, bd_ref, cos_ref, sin_ref,
         q_out, k_out, v_out, u_out) = refs
    else:
        (x_ref, mod_ref, g1_ref, win_ref, qg_ref, kg_ref, bd_ref,
         q_out, k_out, v_out, u_out) = refs
    x = x_ref[...]
    m = mod_ref[...]
    shift1, scale1 = m[0:1], m[1:2]
    h = _rms_rows(x, g1_ref[...]) * (1.0 + scale1) + shift1
    z = _dot(h.astype(BF16), win_ref[...])
    bd = bd_ref[...]
    if rope:
        cos = cos_ref[...]
        sin = sin_ref[...]
        lane = lax.broadcasted_iota(jnp.int32, cos.shape, 1)
        first_half = (lane % HEAD_DIM) < (HEAD_DIM // 2)

    def norm_rope(zc, g):
        zn = zc * lax.rsqrt(_head_sumsq(zc, bd) * (1.0 / HEAD_DIM) + NORM_EPS) * g
        if rope:
            rot = jnp.where(first_half,
                            -pltpu.roll(zn, LANES - HEAD_DIM // 2, 1),
                            pltpu.roll(zn, HEAD_DIM // 2, 1))
            zn = zn * cos + rot * sin
        return zn

    q_scale = HEAD_DIM ** -0.5 * math.log2(math.e)
    for i in range(ATTN_WIDTH // LANES):
        qc = norm_rope(z[:, i * LANES:(i + 1) * LANES], qg_ref[...])
        q_out[:, i * LANES:(i + 1) * LANES] = (qc * q_scale).astype(BF16)
    k_out[...] = norm_rope(z[:, ATTN_WIDTH:ATTN_WIDTH + KV_WIDTH], kg_ref[...])
    v_out[...] = z[:, ATTN_WIDTH + KV_WIDTH:ATTN_WIDTH + 2 * KV_WIDTH]
    u0 = ATTN_WIDTH + 2 * KV_WIDTH
    for gb in range(u_out.shape[0]):
        u_out[gb] = z[:, u0 + gb * LANES:u0 + (gb + 1) * LANES].reshape(u_out.shape[1:])


def _inproj(x, mod, g1, w_in_bf, qg, kg, bd, rope_tabs):
    b, n, d = x.shape
    tm = min(n, 512)
    ssm_w = w_in_bf.shape[1] - ATTN_WIDTH - 2 * KV_WIDTH
    shared = mod.shape[0] == 1
    mod_idx = (lambda i, j: (0, 0, 0)) if shared else (lambda i, j: (i, 0, 0))
    const2 = lambda i, j: (0, 0)
    in_specs = [pl.BlockSpec((None, tm, d), lambda i, j: (i, j, 0)),
                pl.BlockSpec((None, N_MOD, d), mod_idx),
                pl.BlockSpec((1, d), const2),
                pl.BlockSpec(w_in_bf.shape, const2),
                pl.BlockSpec((1, LANES), const2),
                pl.BlockSpec((1, LANES), const2),
                pl.BlockSpec((LANES, LANES), const2)]
    args = [x, mod, g1, w_in_bf, qg, kg, bd]
    if rope_tabs is not None:
        in_specs += [pl.BlockSpec((tm, LANES), lambda i, j: (j, 0))] * 2
        args += list(rope_tabs)
    tok = lambda w: pl.BlockSpec((None, tm, w), lambda i, j: (i, j, 0))
    return pl.pallas_call(
        functools.partial(_inproj_kernel, rope=rope_tabs is not None),
        grid=(b, n // tm),
        in_specs=in_specs,
        out_specs=[tok(ATTN_WIDTH), tok(KV_WIDTH), tok(KV_WIDTH), _chunked_spec(ssm_w, tm)],
        out_shape=[jax.ShapeDtypeStruct((b, n, ATTN_WIDTH), BF16),
                   jax.ShapeDtypeStruct((b, n, KV_WIDTH), F32),
                   jax.ShapeDtypeStruct((b, n, KV_WIDTH), F32),
                   jax.ShapeDtypeStruct((ssm_w // LANES, n // CHUNK, b, CHUNK, LANES), F32)],
        compiler_params=_params(2),
        name="inproj",
    )(*args)


KV_FILL_ROWS = 256
ATTN_Q_ROWS = 512


def _attn_kernel(*refs, n_src):
    q_ref = refs[0]
    kv_refs = refs[1:1 + 2 * n_src]
    out_ref, k4_scr, v4_scr = refs[1 + 2 * n_src:]
    gw = REP * HEAD_DIM

    @pl.when(pl.program_id(1) == 0)
    def _():
        lane = lax.broadcasted_iota(jnp.int32, (KV_FILL_ROWS, KV_WIDTH), 1)
        first = lane < HEAD_DIM
        for dst, srcs in ((k4_scr, kv_refs[0::2]), (v4_scr, kv_refs[1::2])):
            row0 = 0
            for src in srcs:
                for r in range(0, src.shape[0], KV_FILL_ROWS):
                    x = src[r:r + KV_FILL_ROWS, :]
                    sw = pltpu.roll(x, HEAD_DIM, 1)
                    for g, xx in enumerate((jnp.where(first, x, sw), jnp.where(first, sw, x))):
                        xx = xx.astype(BF16)
                        dst[g, row0 + r:row0 + r + KV_FILL_ROWS, :] = jnp.concatenate([xx, xx], axis=1)
                row0 += src.shape[0]

    lane_head = lax.broadcasted_iota(jnp.int32, (q_ref.shape[0], gw), 1) // HEAD_DIM
    for g in range(N_KV_HEADS):
        qg = q_ref[:, g * gw:(g + 1) * gw]
        acc = jnp.zeros((q_ref.shape[0], gw), F32)
        for r in range(REP):
            qm = jnp.where(lane_head == r, qg, jnp.zeros_like(qg))
            s = _dot_nt(qm, k4_scr[g])
            p = jnp.exp2(s - jnp.max(s, axis=-1, keepdims=True))
            l = jnp.sum(p, axis=-1, keepdims=True)
            o = _dot(p.astype(BF16), v4_scr[g])
            acc = jnp.where(lane_head == r, o / l, acc)
        out_ref[:, g * gw:(g + 1) * gw] = acc.astype(BF16)


def _attention(q, kv_sources):
    b, n, _ = q.shape
    s = sum(k.shape[1] for k, _ in kv_sources)
    tq = min(n, ATTN_Q_ROWS)
    gw = REP * HEAD_DIM
    in_specs = [pl.BlockSpec((None, tq, ATTN_WIDTH), lambda i, j: (i, j, 0))]
    args = [q]
    for k, v in kv_sources:
        in_specs += [pl.BlockSpec((None, k.shape[1], KV_WIDTH), lambda i, j: (i, 0, 0))] * 2
        args += [k, v]
    return pl.pallas_call(
        functools.partial(_attn_kernel, n_src=len(kv_sources)),
        grid=(b, n // tq),
        in_specs=in_specs,
        out_specs=pl.BlockSpec((None, tq, ATTN_WIDTH), lambda i, j: (i, j, 0)),
        out_shape=jax.ShapeDtypeStruct((b, n, ATTN_WIDTH), BF16),
        scratch_shapes=[pltpu.VMEM((N_KV_HEADS, s, gw), BF16), pltpu.VMEM((N_KV_HEADS, s, gw), BF16)],
        compiler_params=_params(2),
        name="attn",
    )(*args)


def _cmul(ar, ai, br, bi):
    return ar * br - ai * bi, ar * bi + ai * br


def _cpow(ar, ai, e, shape):
    rr = jnp.ones(shape, F32)
    ri = jnp.zeros(shape, F32)
    pr = jnp.broadcast_to(ar, shape)
    pi = jnp.broadcast_to(ai, shape)
    e = jnp.broadcast_to(e, shape)
    for k in range(CHUNK.bit_length()):
        nr, ni = _cmul(rr, ri, pr, pi)
        bit = ((e >> k) & 1) == 1
        rr = jnp.where(bit, nr, rr)
        ri = jnp.where(bit, ni, ri)
        pr, pi = _cmul(pr, pi, pr, pi)
    return rr, ri


def _zoh(ar, ai, dt):
    mag = jnp.exp(ar * dt)
    abr = mag * jnp.cos(ai * dt)
    abi = mag * jnp.sin(ai * dt)
    den = ar * ar + ai * ai
    nre = abr - 1.0
    nim = abi
    cre = (nre * ar + nim * ai) / den
    cim = (nim * ar - nre * ai) / den
    return abr, abi, cre, cim


def _s5prep_kernel(arc_ref, aic_ref, arr_ref, air_ref, ldt_ref, btr_ref, bti_ref, ctr_ref, cti_ref,
                   arp_ref, aip_ref, ldtp_ref, btpr_ref, btpi_ref,
                   m_out, pre_out, pim_out, qre_out, qim_out, a16r_out, a16i_out):
    P, I, T = SSM_STATE, SSM_GROUP, CHUNK
    W = T * I
    hi = lax.Precision.HIGHEST
    blk = lax.broadcasted_iota(jnp.int32, (1, W), 1) // I
    lane = lax.broadcasted_iota(jnp.int32, (I, W), 1)
    zero_q = jnp.zeros((P, W), F32)
    for gg in range(2):
        kalls = []
        for d in range(2):
            dt = jnp.exp(ldt_ref[d, gg])
            abr_c, abi_c, _, _ = _zoh(arc_ref[d, gg], aic_ref[d, gg], dt)
            _, _, cre_r, cim_r = _zoh(arr_ref[d, gg], air_ref[d, gg], dt)
            bbr, bbi = _cmul(cre_r, cim_r, btr_ref[d, gg], bti_ref[d, gg])

            e_w = blk if d == 0 else (T - 1) - blk
            pwr, pwi = _cpow(abr_c, abi_c, e_w, (P, W))
            w_re, w_im = _cmul(pwr, pwi, ctr_ref[d, gg], cti_ref[d, gg])
            kalls.append(jnp.dot(bbr, w_re, precision=hi, preferred_element_type=F32)
                         - jnp.dot(bbi, w_im, precision=hi, preferred_element_type=F32))

            q_re, q_im = _cmul(w_re, w_im, jnp.broadcast_to(abr_c, (P, W)), jnp.broadcast_to(abi_c, (P, W)))
            pad = lambda q: jnp.concatenate([q, zero_q] if gg == 0 else [zero_q, q], axis=0)
            qre_out[d, gg] = pad(q_re).astype(BF16)
            qim_out[d, gg] = pad(-q_im).astype(BF16)

        for s in range(T):
            fwd = jnp.where(lane >= I * s, pltpu.roll(kalls[0], I * s, 1) if s else kalls[0], 0.0)
            sh = (I * (s + 1)) % W
            bwd = jnp.where(lane < I * (s + 1), pltpu.roll(kalls[1], sh, 1) if sh else kalls[1], 0.0)
            m_out[gg, s * I:(s + 1) * I, :] = (fwd + bwd).astype(BF16)

    srow = lax.broadcasted_iota(jnp.int32, (W, 1), 0) // I
    lane_p = lax.broadcasted_iota(jnp.int32, (W, 2 * P), 1)
    for d in range(2):
        dt = jnp.exp(ldtp_ref[d])
        abr, abi, cre, cim = _zoh(arp_ref[d], aip_ref[d], dt)
        bbr, bbi = _cmul(cre, cim, btpr_ref[d], btpi_ref[d])
        e_p = (T - 1) - srow if d == 0 else srow
        ppr, ppi = _cpow(abr, abi, e_p, (W, 2 * P))
        p_re, p_im = _cmul(ppr, ppi, jnp.concatenate([bbr] * T, axis=0), jnp.concatenate([bbi] * T, axis=0))
        for gg in range(2):
            own = (lane_p < P) if gg == 0 else (lane_p >= P)
            pre_out[d, gg] = jnp.where(own, p_re, 0.0).astype(BF16)
            pim_out[d, gg] = jnp.where(own, p_im, 0.0).astype(BF16)
        a16r, a16i = _cpow(abr, abi, jnp.full((1, 2 * P), T, jnp.int32), (1, 2 * P))
        a16r_out[d] = a16r
        a16i_out[d] = a16i


def _s5_prep(a_re, a_im, log_dt, b_re, b_im, c_re, c_im):
    _, G, P = a_re.shape
    I, T = SSM_GROUP, CHUNK
    W = T * I
    col = lambda a: a.reshape(2, G, P, 1)
    row = lambda a: a.reshape(2, G, 1, P)
    bt = lambda b: jnp.swapaxes(b, 2, 3)
    ct = lambda c: jnp.tile(jnp.swapaxes(c, 2, 3), (1, 1, 1, T))
    prow = lambda a: a.reshape(2, G // 2, 1, 2 * P)
    pbt = lambda b: (bt(b).reshape(2, G // 2, 2, I, P).transpose(0, 1, 3, 2, 4).reshape(2, G // 2, I, 2 * P))
    spec = lambda r, c: pl.BlockSpec((2, 2, r, c), lambda j: (0, j, 0, 0))
    pspec = lambda r, c: pl.BlockSpec((2, None, r, c), lambda j: (0, j, 0, 0))
    shp = lambda r, c, dt: jax.ShapeDtypeStruct((2, G, r, c), dt)
    return pl.pallas_call(
        _s5prep_kernel,
        grid=(G // 2,),
        in_specs=[spec(P, 1), spec(P, 1), spec(1, P), spec(1, P), spec(1, 1),
                  spec(I, P), spec(I, P), spec(P, W), spec(P, W),
                  pspec(1, 2 * P), pspec(1, 2 * P), pspec(1, 2 * P), pspec(I, 2 * P), pspec(I, 2 * P)],
        out_specs=[pl.BlockSpec((2, W, W), lambda j: (j, 0, 0)),
                   spec(W, 2 * P), spec(W, 2 * P), spec(2 * P, W), spec(2 * P, W),
                   pspec(1, 2 * P), pspec(1, 2 * P)],
        out_shape=[jax.ShapeDtypeStruct((G, W, W), BF16),
                   shp(W, 2 * P, BF16), shp(W, 2 * P, BF16), shp(2 * P, W, BF16), shp(2 * P, W, BF16),
                   jax.ShapeDtypeStruct((2, G // 2, 1, 2 * P), F32),
                   jax.ShapeDtypeStruct((2, G // 2, 1, 2 * P), F32)],
        compiler_params=_params(1),
        name="s5prep",
    )(col(a_re), col(a_im), row(a_re), row(a_im), log_dt.reshape(2, G, 1, 1),
      bt(b_re), bt(b_im), ct(c_re), ct(c_im),
      prow(a_re), prow(a_im), prow(jnp.repeat(log_dt, P, axis=-1)), pbt(b_re), pbt(b_im))


RELAY_ROWS = 16
S5_OUT_ROWS = 256


def _block_transpose(xs, masks):
    n = len(xs)
    rolled = []
    for delta in range(n):
        comp = xs[delta]
        for q in range(1, n):
            comp = jnp.where(masks[q], xs[(q + delta) % n], comp)
        rolled.append(pltpu.roll(comp, SSM_GROUP * delta, 1) if delta else comp)
    out = []
    for a in range(n):
        y = rolled[(-a) % n]
        for p in range(1, n):
            y = jnp.where(masks[p], rolled[(p - a) % n], y)
        out.append(y)
    return out


def _s5_kernel(*refs, batch, has_h0):
    if has_h0:
        (u_ref, m_ref, pre_ref, pim_ref, qre_ref, qim_ref, a16r_ref, a16i_ref, h0_ref,
         y_ref, fin_ref, z_scr, s_scr, yt_scr) = refs
    else:
        (u_ref, m_ref, pre_ref, pim_ref, qre_ref, qim_ref, a16r_ref, a16i_ref,
         y_ref, fin_ref, z_scr, s_scr, yt_scr) = refs
    I, T, RT, RC = SSM_GROUP, CHUNK, RELAY_ROWS, S5_OUT_ROWS
    gpb = LANES // I
    halves = T * I // LANES
    rows = u_ref.shape[0] // T
    n_chunks = rows // batch
    lane_blk = lax.broadcasted_iota(jnp.int32, (RT, LANES), 1) // I
    masks = [lane_blk == p for p in range(gpb)]

    def to_groups(i, carry):
        r0 = pl.multiple_of(i * RT, RT)
        for h in range(halves):
            xs = [u_ref[pl.ds(r0 * T + p + gpb * h, RT, stride=T), :] for p in range(gpb)]
            for g, zg in enumerate(_block_transpose(xs, masks)):
                z_scr[g, pl.ds(r0, RT), h * LANES:(h + 1) * LANES] = zg.astype(BF16)
        return carry

    lax.fori_loop(0, rows // RT, to_groups, 0)

    for j in range(gpb // 2):
        for d in range(2):
            for c, pref in enumerate((pre_ref, pim_ref)):
                s_scr[j, d, c] = _dot(z_scr[2 * j], pref[d, 2 * j]) + _dot(z_scr[2 * j + 1], pref[d, 2 * j + 1])
        a = [(jnp.broadcast_to(a16r_ref[d, j], (batch, LANES)),
              jnp.broadcast_to(a16i_ref[d, j], (batch, LANES))) for d in range(2)]
        if has_h0:
            init = tuple(h0_ref[j, d, c] for d in range(2) for c in range(2))
        else:
            init = tuple(jnp.zeros((batch, LANES), F32) for _ in range(4))

        def step(k, carry):
            out = []
            for d in range(2):
                hr, hi = carry[2 * d], carry[2 * d + 1]
                c = k if d == 0 else n_chunks - 1 - k
                rs = pl.ds(pl.multiple_of(c * batch, batch), batch)
                sr = s_scr[j, d, 0, rs, :]
                si = s_scr[j, d, 1, rs, :]
                s_scr[j, d, 0, rs, :] = hr
                s_scr[j, d, 1, rs, :] = hi
                ar, ai = a[d]
                out.append(ar * hr - ai * hi + sr)
                out.append(ar * hi + ai * hr + si)
            return tuple(out)

        fin = lax.fori_loop(0, n_chunks, step, init)
        for d in range(2):
            fin_ref[j, d, 0] = fin[2 * d]
            fin_ref[j, d, 1] = fin[2 * d + 1]

    def out_tile(i, carry):
        r0 = pl.multiple_of(i * RC, RC)
        rs = pl.ds(r0, RC)
        for g in range(gpb):
            j = g // 2
            y = _dot(z_scr[g, rs, :], m_ref[g])
            for d in range(2):
                y = (y + _dot(s_scr[j, d, 0, rs, :].astype(BF16), qre_ref[d, g])
                     + _dot(s_scr[j, d, 1, rs, :].astype(BF16), qim_ref[d, g]))
            yt_scr[g] = y

        def to_tokens(k, carry2):
            q0 = pl.multiple_of(k * RT, RT)
            for h in range(halves):
                xs = [yt_scr[g, pl.ds(q0, RT), h * LANES:(h + 1) * LANES] for g in range(gpb)]
                for p, yp in enumerate(_block_transpose(xs, masks)):
                    y_ref[pl.ds((r0 + q0) * T + p + gpb * h, RT, stride=T), :] = yp
            return carry2

        lax.fori_loop(0, RC // RT, to_tokens, 0)
        return carry

    lax.fori_loop(0, rows // RC, out_tile, 0)


def _s5_scan(u_blk, ops, h0):
    nb, nc, b, _, _ = u_blk.shape
    gpb = LANES // SSM_GROUP
    npair = gpb // 2
    W = CHUNK * SSM_GROUP
    P = SSM_STATE
    rows = nc * b
    m, pre, pim, qre, qim, a16r, a16i = ops
    spec = lambda r, c: pl.BlockSpec((2, gpb, r, c), lambda j: (0, j, 0, 0))
    pspec = pl.BlockSpec((2, npair, 1, 2 * P), lambda j: (0, j, 0, 0))
    act = lambda **kw: pl.BlockSpec((None, rows * CHUNK, LANES), lambda j: (j, 0, 0), **kw)
    st = pl.BlockSpec((None, npair, 2, 2, b, 2 * P), lambda j: (j, 0, 0, 0, 0, 0))
    in_specs = [act(pipeline_mode=pl.Buffered(1)), pl.BlockSpec((gpb, W, W), lambda j: (j, 0, 0)),
                spec(W, 2 * P), spec(W, 2 * P), spec(2 * P, W), spec(2 * P, W), pspec, pspec]
    args = [u_blk.reshape(nb, rows * CHUNK, LANES), m, pre, pim, qre, qim, a16r, a16i]
    if h0 is not None:
        in_specs.append(st)
        args.append(h0.reshape(b, 2, 2, nb, npair, 2, P).transpose(3, 4, 1, 2, 0, 5, 6)
                    .reshape(nb, npair, 2, 2, b, 2 * P))
    y, fin = pl.pallas_call(
        functools.partial(_s5_kernel, batch=b, has_h0=h0 is not None),
        grid=(nb,),
        in_specs=in_specs,
        out_specs=[act(), st],
        out_shape=[jax.ShapeDtypeStruct((nb, rows * CHUNK, LANES), F32),
                   jax.ShapeDtypeStruct((nb, npair, 2, 2, b, 2 * P), F32)],
        scratch_shapes=[pltpu.VMEM((gpb, rows, W), BF16),
                        pltpu.VMEM((npair, 2, 2, rows, 2 * P), F32),
                        pltpu.VMEM((gpb, S5_OUT_ROWS, W), F32)],
        compiler_params=_params(1),
        name="s5",
    )(*args)
    fin = (fin.reshape(nb, npair, 2, 2, b, 2, P).transpose(4, 2, 3, 0, 1, 5, 6)
           .reshape(b, 2, 2, nb * gpb, P))
    return y.reshape(u_blk.shape), fin


def _post_kernel(x_ref, attn_ref, y_ref, u_ref, d_ref, wglu_ref, bglu_ref, wout_ref, mod_ref, g2_ref,
                 wrt_ref, x1_out, h2_out, aff_out):
    m = mod_ref[...]
    gate1, shift2, scale2 = m[2:3], m[3:4], m[4:5]
    tm = x_ref.shape[0]
    tokens = lambda r: jnp.concatenate([r[gb].reshape(tm, LANES) for gb in range(r.shape[0])], axis=1)
    y = tokens(y_ref) + tokens(u_ref) * d_ref[...]
    y = jax.nn.gelu(y)
    y = y * jax.nn.sigmoid(_dot(y.astype(BF16), wglu_ref[...]) + bglu_ref[...])
    aw = attn_ref.shape[1]
    proj = _dot(attn_ref[...], wout_ref[:aw, :]) + _dot(y.astype(BF16), wout_ref[aw:, :])
    x1 = x_ref[...] + gate1 * proj
    x1_out[...] = x1
    h2 = (_rms_rows(x1, g2_ref[...]) * (1.0 + scale2) + shift2).astype(BF16)
    h2_out[...] = h2
    logits = _dot_nt(wrt_ref[...], h2)
    e = jnp.exp(logits - jnp.max(logits, axis=0, keepdims=True))
    aff_out[...] = e / jnp.sum(e, axis=0, keepdims=True)


def _post(x, attn, y_ssm, u, d_skip, wglu_bf, b_glu, wout_bf, mod, g2, wrt_bf):
    b, n, d = x.shape
    tm = min(n, 512)
    sw = u.shape[0] * LANES
    shared = mod.shape[0] == 1
    mod_idx = (lambda i, j: (0, 0, 0)) if shared else (lambda i, j: (i, 0, 0))
    const2 = lambda i, j: (0, 0)
    tok = lambda w: pl.BlockSpec((None, tm, w), lambda i, j: (i, j, 0))
    return pl.pallas_call(
        _post_kernel,
        grid=(b, n // tm),
        in_specs=[tok(d), tok(attn.shape[-1]), _chunked_spec(sw, tm), _chunked_spec(sw, tm),
                  pl.BlockSpec((1, sw), const2),
                  pl.BlockSpec(wglu_bf.shape, const2),
                  pl.BlockSpec((1, sw), const2),
                  pl.BlockSpec(wout_bf.shape, const2),
                  pl.BlockSpec((None, N_MOD, d), mod_idx),
                  pl.BlockSpec((1, d), const2),
                  pl.BlockSpec(wrt_bf.shape, const2)],
        out_specs=[tok(d), tok(d), pl.BlockSpec((None, N_EXPERTS, tm), lambda i, j: (i, 0, j))],
        out_shape=[jax.ShapeDtypeStruct((b, n, d), F32),
                   jax.ShapeDtypeStruct((b, n, d), BF16),
                   jax.ShapeDtypeStruct((b, N_EXPERTS, n), F32)],
        compiler_params=_params(2),
        name="post",
    )(x, attn, y_ssm, u, d_skip, wglu_bf, b_glu, wout_bf, mod, g2, wrt_bf)


def _route_kernel(aff_ref, pos_out, *, cap):
    a = aff_ref[...]
    rows, n = a.shape
    capf = jnp.float32(cap)

    def count(mask):
        return jnp.sum(jnp.where(mask, 1.0, 0.0), axis=1, keepdims=True)

    def search(i, bits):
        cand = bits | (jnp.int32(1) << (30 - i))
        ok = count(a >= lax.bitcast_convert_type(cand, F32)) >= capf
        return jnp.where(ok, cand, bits)

    thr_bits = lax.fori_loop(0, 31, search, jnp.zeros((rows, 1), jnp.int32))
    thr = lax.bitcast_convert_type(thr_bits, F32)
    gt = a > thr
    eq = a == thr
    before = (lax.broadcasted_iota(jnp.int32, (n, n), 0)
              < lax.broadcasted_iota(jnp.int32, (n, n), 1))
    tri = jnp.where(before, 1.0, 0.0).astype(BF16)
    eq_rank = _dot(jnp.where(eq, 1.0, 0.0).astype(BF16), tri)
    sel = gt | (eq & (eq_rank < capf - count(gt)))
    pos = _dot(jnp.where(sel, 1.0, 0.0).astype(BF16), tri)
    pos_out[...] = jnp.where(sel, pos, -1.0).astype(jnp.int32)


def _route(aff_t, cap):
    b, e, n = aff_t.shape
    rows = b * e
    pos = pl.pallas_call(
        functools.partial(_route_kernel, cap=cap),
        grid=(1,),
        in_specs=[pl.BlockSpec((rows, n), lambda i: (0, 0))],
        out_specs=pl.BlockSpec((rows, n), lambda i: (0, 0)),
        out_shape=jax.ShapeDtypeStruct((rows, n), jnp.int32),
        compiler_params=_params(1),
        name="route",
    )(aff_t.reshape(rows, n))
    return pos.reshape(b, e, n)


def _gather_kernel(h_ref, pos_ref, aff_ref, x_out, gate_out, p_scr, *, cap, group):
    n = h_ref.shape[0]
    slot = lax.broadcasted_iota(jnp.int32, (cap, n), 0)
    for e0 in range(0, N_EXPERTS, group):
        for j in range(group):
            e = e0 + j
            hit = pos_ref[e:e + 1, :] == slot
            p_scr[j * cap:(j + 1) * cap, :] = jnp.where(hit, 1.0, 0.0).astype(BF16)
            gate_out[e] = jnp.sum(jnp.where(hit, aff_ref[e:e + 1, :], 0.0), axis=1, keepdims=True)
        xs = _dot(p_scr[...], h_ref[...]).astype(BF16)
        for j in range(group):
            x_out[e0 + j] = xs[j * cap:(j + 1) * cap, :]


def _gather(h2, pos, aff_t, cap):
    b, n, d = h2.shape
    group = max(1, min(N_EXPERTS, MOE_ROWS // cap))
    return pl.pallas_call(
        functools.partial(_gather_kernel, cap=cap, group=group),
        grid=(b,),
        in_specs=[pl.BlockSpec((None, n, d), lambda i: (i, 0, 0)),
                  pl.BlockSpec((None, N_EXPERTS, n), lambda i: (i, 0, 0)),
                  pl.BlockSpec((None, N_EXPERTS, n), lambda i: (i, 0, 0))],
        out_specs=[pl.BlockSpec((None, N_EXPERTS, cap, d), lambda i: (i, 0, 0, 0)),
                   pl.BlockSpec((None, N_EXPERTS, cap, 1), lambda i: (i, 0, 0, 0))],
        out_shape=[jax.ShapeDtypeStruct((b, N_EXPERTS, cap, d), BF16),
                   jax.ShapeDtypeStruct((b, N_EXPERTS, cap, 1), F32)],
        scratch_shapes=[pltpu.VMEM((group * cap, n), BF16)],
        compiler_params=_params(1),
        name="gather",
    )(h2, pos, aff_t)


FFN_SPLIT = 4


def _ffn_kernel(x_ref, gate_ref, wg_ref, wu_ref, wd_ref, y_out):
    bt, cap, d = x_ref.shape
    fs = wg_ref.shape[1] // FFN_SPLIT
    x = x_ref[...].reshape(bt * cap, d)
    y = jnp.zeros((bt * cap, d), F32)
    for h in range(FFN_SPLIT):
        cols = slice(h * fs, (h + 1) * fs)
        g = _dot(x, wg_ref[:, cols].astype(BF16))
        hid = (g * jax.nn.sigmoid(g)) * _dot(x, wu_ref[:, cols].astype(BF16))
        y = y + _dot(hid.astype(BF16), wd_ref[cols, :].astype(BF16))
    y = y * gate_ref[...].reshape(bt * cap, 1)
    y_out[...] = y.astype(BF16).reshape(bt, cap, d)


def _ffn(xsel, gate, wg, wu, wd):
    b, e, cap, d = xsel.shape
    bt = max(1, min(b, MOE_ROWS // cap))
    f = wg.shape[-1]
    return pl.pallas_call(
        _ffn_kernel,
        grid=(e, b // bt),
        in_specs=[pl.BlockSpec((bt, None, cap, d), lambda i, j: (j, i, 0, 0)),
                  pl.BlockSpec((bt, None, cap, 1), lambda i, j: (j, i, 0, 0)),
                  pl.BlockSpec((None, d, f), lambda i, j: (i, 0, 0)),
                  pl.BlockSpec((None, d, f), lambda i, j: (i, 0, 0)),
                  pl.BlockSpec((None, f, d), lambda i, j: (i, 0, 0))],
        out_specs=pl.BlockSpec((bt, None, cap, d), lambda i, j: (j, i, 0, 0)),
        out_shape=jax.ShapeDtypeStruct((b, e, cap, d), BF16),
        compiler_params=_params(2),
        name="ffn",
    )(xsel, gate, wg, wu, wd)


def _combine_kernel(x1_ref, y_ref, post_ref, mod_ref, gf_ref, out_ref, *, cap):
    tn = x1_ref.shape[0]
    e, _, d = y_ref.shape
    width = min(e * cap, 512)
    per = width // cap
    post = post_ref[...].astype(F32).astype(BF16)
    lane = lax.broadcasted_iota(jnp.int32, (e, width), 1)
    erow = lax.broadcasted_iota(jnp.int32, (e, width), 0)
    slot = (lax.broadcasted_iota(jnp.int32, (tn, width), 1) % cap).astype(F32)
    acc = jnp.zeros((tn, d), F32)
    for c in range(e // per):
        expand = jnp.where(erow == c * per + lane // cap, 1.0, 0.0).astype(BF16)
        hit = _dot(post, expand) == slot
        ys = y_ref[c * per:(c + 1) * per].reshape(width, d)
        acc = acc + _dot(jnp.where(hit, 1.0, 0.0).astype(BF16), ys)
    gate2 = mod_ref[...][5:6]
    out_ref[...] = _rms_rows(x1_ref[...] + gate2 * acc, gf_ref[...])


def _combine(x1, ysel, pos_t, mod, gf, cap):
    b, n, d = x1.shape
    tn = min(n, MOE_ROWS)
    shared = mod.shape[0] == 1
    mod_idx = (lambda i, j: (0, 0, 0)) if shared else (lambda i, j: (i, 0, 0))
    return pl.pallas_call(
        functools.partial(_combine_kernel, cap=cap),
        grid=(b, n // tn),
        in_specs=[pl.BlockSpec((None, tn, d), lambda i, j: (i, j, 0)),
                  pl.BlockSpec((None, N_EXPERTS, cap, d), lambda i, j: (i, 0, 0, 0)),
                  pl.BlockSpec((None, tn, N_EXPERTS), lambda i, j: (i, j, 0)),
                  pl.BlockSpec((None, N_MOD, d), mod_idx),
                  pl.BlockSpec((1, d), lambda i, j: (0, 0))],
        out_specs=pl.BlockSpec((None, tn, d), lambda i, j: (i, j, 0)),
        out_shape=jax.ShapeDtypeStruct((b, n, d), F32),
        compiler_params=_params(2),
        name="combine",
    )(x1, ysel, pos_t, mod, gf)


def _rope_tables(n_tokens):
    pairs = HEAD_DIM // 4
    rows = n_tokens // GRID_W
    row = jnp.repeat(jnp.arange(rows, dtype=F32), GRID_W)
    col = jnp.tile(jnp.arange(GRID_W, dtype=F32), rows)
    inv_freq = ROPE_THETA ** (-jnp.arange(pairs, dtype=F32) / pairs)
    ang = jnp.concatenate([row[:, None] * inv_freq, col[:, None] * inv_freq], axis=-1)
    reps = LANES // (HEAD_DIM // 2)
    return jnp.tile(jnp.cos(ang), (1, reps)), jnp.tile(jnp.sin(ang), (1, reps))


def _trunk(x, mod, w, rope_tabs, ctx_k, ctx_v, ctx_state):
    b, n, d = x.shape
    q, k, v, u = _inproj(x, mod, w["g1"], w["w_in"], w["qg"], w["kg"], w["bd"], rope_tabs)
    kv = [(k, v)]
    if ctx_k is not None:
        kv.append((ctx_k.reshape(b, -1, KV_WIDTH), ctx_v.reshape(b, -1, KV_WIDTH)))
    attn = _attention(q, kv)
    y_ssm, state = _s5_scan(u, w["s5ops"], ctx_state)
    x1, h2, aff_t = _post(x, attn, y_ssm, u, w["d_skip"], w["w_glu"], w["b_glu"], w["w_out"], mod,
                          w["g2"], w["w_router_t"])
    cap = CAPACITY_FACTOR * n // N_EXPERTS
    pos = _route(aff_t, cap)
    xsel, gate = _gather(h2, pos, aff_t, cap)
    ysel = _ffn(xsel, gate, w["w_gate"], w["w_up"], w["w_down"])
    out = _combine(x1, ysel, pos.transpose(0, 2, 1), mod, w["gf"], cap)
    return out, k, v, state


def kernel(x_prompt, x_sample, cache_k, cache_v, state_ssm, c, c_ctx, norm1_g, norm2_g, w_mod, b_mod, w_in, q_norm_g, k_norm_g, ssm_a_re, ssm_a_im, ssm_log_dt, ssm_b_re, ssm_b_im, ssm_c_re, ssm_c_im, ssm_d, w_glu, b_glu, w_out, w_router, w_gate, w_up, w_down, final_norm_g):
    assert norm1_g.shape[0] == 1, "single trunk layer"
    d = x_prompt.shape[-1]
    n_dec = c.shape[0]
    cond = jnp.concatenate([c_ctx[None, :], c, jnp.zeros((16 - 1 - n_dec, d), F32)], axis=0)
    mod = _modulation(cond, w_mod[0], b_mod[0]).reshape(16, N_MOD, d)
    head = jnp.arange(LANES) // HEAD_DIM
    w = {
        "g1": norm1_g, "g2": norm2_g, "gf": final_norm_g.reshape(1, d),
        "w_in": w_in[0].astype(BF16),
        "qg": jnp.tile(q_norm_g, (1, LANES // HEAD_DIM)),
        "kg": jnp.tile(k_norm_g, (1, LANES // HEAD_DIM)),
        "bd": (head[:, None] == head[None, :]).astype(BF16),
        "s5ops": _s5_prep(ssm_a_re[0], ssm_a_im[0], ssm_log_dt[0], ssm_b_re[0], ssm_b_im[0],
                          ssm_c_re[0], ssm_c_im[0]),
        "d_skip": ssm_d, "w_glu": w_glu[0].astype(BF16), "b_glu": b_glu,
        "w_out": w_out[0].astype(BF16),
        "w_router_t": w_router[0].T.astype(BF16),
        "w_gate": w_gate[0], "w_up": w_up[0], "w_down": w_down[0],
    }
    y_p, k_ctx, v_ctx, st_ctx = _trunk(x_prompt, mod[0:1], w, None, None, None, None)
    y_s, _, _, _ = _trunk(x_sample, mod[1:1 + n_dec], w, _rope_tables(x_sample.shape[1]),
                          cache_k[:, 0], cache_v[:, 0], state_ssm[:, 0])
    bp, np_ = x_prompt.shape[:2]
    new_k = k_ctx.reshape(bp, 1, np_, N_KV_HEADS, HEAD_DIM)
    new_v = v_ctx.reshape(bp, 1, np_, N_KV_HEADS, HEAD_DIM)
    return (y_p, y_s, new_k, new_v, st_ctx[:, None])
```

```python
import functools
import math

import jax
import jax.numpy as jnp
from jax import lax
from jax.experimental import pallas as pl
from jax.experimental.pallas import tpu as pltpu

F32 = jnp.float32
BF16 = jnp.bfloat16

HEAD_DIM = 64
N_HEADS = 8
N_KV_HEADS = 2
REP = N_HEADS // N_KV_HEADS
ATTN_WIDTH = N_HEADS * HEAD_DIM
KV_WIDTH = N_KV_HEADS * HEAD_DIM
SSM_GROUP = 16
SSM_STATE = 64
N_EXPERTS = 16
CAPACITY_FACTOR = 2
GRID_W = 64
ROPE_THETA = 10000.0
NORM_EPS = 1e-6
N_MOD = 6
CHUNK = 16
LANES = 128
VMEM_LIMIT = 56 * 1024 * 1024
MOE_ROWS = 1024


def _params(n_grid_axes, vmem=VMEM_LIMIT):
    return pltpu.CompilerParams(dimension_semantics=("arbitrary",) * n_grid_axes,
                                vmem_limit_bytes=vmem)


def _chunked_spec(width, tm):
    return pl.BlockSpec((width // LANES, tm // CHUNK, None, CHUNK, LANES), lambda i, j: (0, j, i, 0, 0))


def _dot(a, b):
    return jnp.dot(a, b, preferred_element_type=F32)


def _dot_nt(a, b):
    return lax.dot_general(a, b, (((1,), (1,)), ((), ())), preferred_element_type=F32)


def _rms_rows(x, g):
    return x * lax.rsqrt(jnp.mean(x * x, axis=-1, keepdims=True) + NORM_EPS) * g


def _mod_kernel(cond_ref, w_ref, b_ref, out_ref):
    c = cond_ref[...]
    s = (c * jax.nn.sigmoid(c)).astype(BF16)
    out_ref[...] = _dot(s, w_ref[...].astype(BF16)) + b_ref[...]


def _modulation(cond, w_mod, b_mod):
    rows, d = cond.shape
    n = w_mod.shape[1]
    tn = 1024
    return pl.pallas_call(
        _mod_kernel,
        grid=(n // tn,),
        in_specs=[pl.BlockSpec((rows, d), lambda j: (0, 0)),
                  pl.BlockSpec((d, tn), lambda j: (0, j)),
                  pl.BlockSpec((1, tn), lambda j: (0, j))],
        out_specs=pl.BlockSpec((rows, tn), lambda j: (0, j)),
        out_shape=jax.ShapeDtypeStruct((rows, n), F32),
        compiler_params=_params(1),
        name="mod",
    )(cond, w_mod, b_mod.reshape(1, n))


def _inproj_kernel(*refs, rope):
    if rope:
        (x_ref, mod_ref, g1_ref, win_ref, qg_ref, kg_ref, bd_ref, cos_ref, sin_ref,
         q_out, k_out, v_out, u_out) = refs
    else:
        (x_ref, mod_ref, g1_ref, win_ref, qg_ref, kg_ref, bd_ref,
         q_out, k_out, v_out, u_out) = refs
    x = x_ref[...]
    m = mod_ref[...]
    shift1, scale1 = m[0:1], m[1:2]
    h = _rms_rows(x, g1_ref[...]) * (1.0 + scale1) + shift1
    z = _dot(h.astype(BF16), win_ref[...])
    bd = bd_ref[...]
    if rope:
        cos = cos_ref[...]
        sin = sin_ref[...]
        lane = lax.broadcasted_iota(jnp.int32, cos.shape, 1)
        first_half = (lane % HEAD_DIM) < (HEAD_DIM // 2)

    def head_norm(zc, g):
        w = zc.shape[1]
        ss = _dot((zc * zc).astype(BF16), bd[:w, :w])
        return zc * lax.rsqrt(ss * (1.0 / HEAD_DIM) + NORM_EPS) * g[:, :w]

    def rope_lanes(zn):
        if not rope:
            return zn
        rot = jnp.where(first_half,
                        -pltpu.roll(zn, LANES - HEAD_DIM // 2, 1),
                        pltpu.roll(zn, HEAD_DIM // 2, 1))
        return zn * cos + rot * sin

    q_scale = HEAD_DIM ** -0.5 * math.log2(math.e)
    nw = bd.shape[0]
    for i in range(0, ATTN_WIDTH, nw):
        qn = head_norm(z[:, i:i + nw], qg_ref[...])
        for j in range(0, nw, LANES):
            q_out[:, i + j:i + j + LANES] = (rope_lanes(qn[:, j:j + LANES]) * q_scale).astype(BF16)
    k_out[...] = rope_lanes(head_norm(z[:, ATTN_WIDTH:ATTN_WIDTH + KV_WIDTH], kg_ref[...]))
    v_out[...] = z[:, ATTN_WIDTH + KV_WIDTH:ATTN_WIDTH + 2 * KV_WIDTH]
    u0 = ATTN_WIDTH + 2 * KV_WIDTH
    for gb in range(u_out.shape[0]):
        u_out[gb] = z[:, u0 + gb * LANES:u0 + (gb + 1) * LANES].reshape(u_out.shape[1:])


def _inproj(x, mod, g1, w_in_bf, qg, kg, bd, rope_tabs):
    b, n, d = x.shape
    tm = min(n, 512)
    ssm_w = w_in_bf.shape[1] - ATTN_WIDTH - 2 * KV_WIDTH
    shared = mod.shape[0] == 1
    mod_idx = (lambda i, j: (0, 0, 0)) if shared else (lambda i, j: (i, 0, 0))
    const2 = lambda i, j: (0, 0)
    in_specs = [pl.BlockSpec((None, tm, d), lambda i, j: (i, j, 0)),
                pl.BlockSpec((None, N_MOD, d), mod_idx),
                pl.BlockSpec((1, d), const2),
                pl.BlockSpec(w_in_bf.shape, const2),
                pl.BlockSpec(qg.shape, const2),
                pl.BlockSpec(kg.shape, const2),
                pl.BlockSpec(bd.shape, const2)]
    args = [x, mod, g1, w_in_bf, qg, kg, bd]
    if rope_tabs is not None:
        in_specs += [pl.BlockSpec((tm, LANES), lambda i, j: (j, 0))] * 2
        args += list(rope_tabs)
    tok = lambda w: pl.BlockSpec((None, tm, w), lambda i, j: (i, j, 0))
    return pl.pallas_call(
        functools.partial(_inproj_kernel, rope=rope_tabs is not None),
        grid=(b, n // tm),
        in_specs=in_specs,
        out_specs=[tok(ATTN_WIDTH), tok(KV_WIDTH), tok(KV_WIDTH), _chunked_spec(ssm_w, tm)],
        out_shape=[jax.ShapeDtypeStruct((b, n, ATTN_WIDTH), BF16),
                   jax.ShapeDtypeStruct((b, n, KV_WIDTH), F32),
                   jax.ShapeDtypeStruct((b, n, KV_WIDTH), F32),
                   jax.ShapeDtypeStruct((ssm_w // LANES, n // CHUNK, b, CHUNK, LANES), F32)],
        compiler_params=_params(2),
        name="inproj",
    )(*args)


KV_FILL_ROWS = 256
ATTN_Q_ROWS = 512
ATTN_KEY_SLAB = 256


def _attn_kernel(*refs, n_src):
    q_ref = refs[0]
    kv_refs = refs[1:1 + 2 * n_src]
    out_ref, k4_scr, vt_scr = refs[1 + 2 * n_src:]
    gw = REP * HEAD_DIM

    @pl.when(pl.program_id(1) == 0)
    def _():
        lane = lax.broadcasted_iota(jnp.int32, (KV_FILL_ROWS, KV_WIDTH), 1)
        first = lane < HEAD_DIM
        row0 = 0
        for k_src, v_src in zip(kv_refs[0::2], kv_refs[1::2]):
            for r in range(0, k_src.shape[0], KV_FILL_ROWS):
                rows = slice(row0 + r, row0 + r + KV_FILL_ROWS)
                x = k_src[r:r + KV_FILL_ROWS, :]
                sw = pltpu.roll(x, HEAD_DIM, 1)
                for g, xx in enumerate((jnp.where(first, x, sw), jnp.where(first, sw, x))):
                    xx = xx.astype(BF16)
                    k4_scr[g, rows, :] = jnp.concatenate([xx, xx], axis=1)
                vt = v_src[r:r + KV_FILL_ROWS, :].T
                for g in range(N_KV_HEADS):
                    vt_scr[g, :, rows] = vt[g * HEAD_DIM:(g + 1) * HEAD_DIM, :].astype(BF16)
            row0 += k_src.shape[0]

    lane_head = lax.broadcasted_iota(jnp.int32, (q_ref.shape[0], gw), 1) // HEAD_DIM

    n_keys = k4_scr.shape[1]
    slab = min(n_keys, ATTN_KEY_SLAB)

    def scores(h):
        g, r = divmod(h, REP)
        qg = q_ref[:, g * gw:(g + 1) * gw]
        qm = jnp.where(lane_head == r, qg, jnp.zeros_like(qg))
        sts = [_dot_nt(k4_scr[g, i:i + slab, :], qm) for i in range(0, n_keys, slab)]
        m = functools.reduce(jnp.maximum, [jnp.max(s, axis=0, keepdims=True) for s in sts])
        return sts, m

    heads = []
    nxt = scores(0)
    for h in range(N_HEADS):
        sts, m = nxt
        if h + 1 < N_HEADS:
            nxt = scores(h + 1)
        l = jnp.zeros_like(m)
        o = jnp.zeros((HEAD_DIM, q_ref.shape[0]), F32)
        for i, st in enumerate(sts):
            pt = jnp.exp2(st - m)
            l = l + jnp.sum(pt, axis=0, keepdims=True)
            o = o + _dot(vt_scr[h // REP, :, i * slab:(i + 1) * slab], pt.astype(BF16))
        heads.append(o / l)
    for i in range(0, N_HEADS, 2):
        pair = jnp.concatenate(heads[i:i + 2], axis=0)
        out_ref[:, i * HEAD_DIM:(i + 2) * HEAD_DIM] = pair.T.astype(BF16)


def _attention(q, kv_sources):
    b, n, _ = q.shape
    s = sum(k.shape[1] for k, _ in kv_sources)
    tq = min(n, ATTN_Q_ROWS)
    gw = REP * HEAD_DIM
    in_specs = [pl.BlockSpec((None, tq, ATTN_WIDTH), lambda i, j: (i, j, 0))]
    args = [q]
    for k, v in kv_sources:
        in_specs += [pl.BlockSpec((None, k.shape[1], KV_WIDTH), lambda i, j: (i, 0, 0))] * 2
        args += [k, v]
    return pl.pallas_call(
        functools.partial(_attn_kernel, n_src=len(kv_sources)),
        grid=(b, n // tq),
        in_specs=in_specs,
        out_specs=pl.BlockSpec((None, tq, ATTN_WIDTH), lambda i, j: (i, j, 0)),
        out_shape=jax.ShapeDtypeStruct((b, n, ATTN_WIDTH), BF16),
        scratch_shapes=[pltpu.VMEM((N_KV_HEADS, s, gw), BF16), pltpu.VMEM((N_KV_HEADS, HEAD_DIM, s), BF16)],
        compiler_params=_params(2),
        name="attn",
    )(*args)


def _cmul(ar, ai, br, bi):
    return ar * br - ai * bi, ar * bi + ai * br


def _cpow(ar, ai, e, shape):
    rr = jnp.ones(shape, F32)
    ri = jnp.zeros(shape, F32)
    pr = jnp.broadcast_to(ar, shape)
    pi = jnp.broadcast_to(ai, shape)
    e = jnp.broadcast_to(e, shape)
    for k in range(CHUNK.bit_length()):
        nr, ni = _cmul(rr, ri, pr, pi)
        bit = ((e >> k) & 1) == 1
        rr = jnp.where(bit, nr, rr)
        ri = jnp.where(bit, ni, ri)
        pr, pi = _cmul(pr, pi, pr, pi)
    return rr, ri


def _zoh(ar, ai, dt):
    mag = jnp.exp(ar * dt)
    abr = mag * jnp.cos(ai * dt)
    abi = mag * jnp.sin(ai * dt)
    den = ar * ar + ai * ai
    nre = abr - 1.0
    nim = abi
    cre = (nre * ar + nim * ai) / den
    cim = (nim * ar - nre * ai) / den
    return abr, abi, cre, cim


def _s5prep_kernel(arc_ref, aic_ref, arr_ref, air_ref, ldt_ref, btr_ref, bti_ref, ctr_ref, cti_ref,
                   arp_ref, aip_ref, ldtp_ref, btpr_ref, btpi_ref,
                   m_out, pre_out, pim_out, qre_out, qim_out, a16r_out, a16i_out):
    P, I, T = SSM_STATE, SSM_GROUP, CHUNK
    W = T * I
    hi = lax.Precision.HIGHEST
    blk = lax.broadcasted_iota(jnp.int32, (1, W), 1) // I
    lane = lax.broadcasted_iota(jnp.int32, (I, W), 1)
    zero_q = jnp.zeros((P, W), F32)
    for gg in range(2):
        kalls = []
        for d in range(2):
            dt = jnp.exp(ldt_ref[d, gg])
            abr_c, abi_c, _, _ = _zoh(arc_ref[d, gg], aic_ref[d, gg], dt)
            _, _, cre_r, cim_r = _zoh(arr_ref[d, gg], air_ref[d, gg], dt)
            bbr, bbi = _cmul(cre_r, cim_r, btr_ref[d, gg], bti_ref[d, gg])

            e_w = blk if d == 0 else (T - 1) - blk
            pwr, pwi = _cpow(abr_c, abi_c, e_w, (P, W))
            w_re, w_im = _cmul(pwr, pwi, ctr_ref[d, gg], cti_ref[d, gg])
            kalls.append(jnp.dot(bbr, w_re, precision=hi, preferred_element_type=F32)
                         - jnp.dot(bbi, w_im, precision=hi, preferred_element_type=F32))

            q_re, q_im = _cmul(w_re, w_im, jnp.broadcast_to(abr_c, (P, W)), jnp.broadcast_to(abi_c, (P, W)))
            pad = lambda q: jnp.concatenate([q, zero_q] if gg == 0 else [zero_q, q], axis=0)
            qre_out[d, gg] = pad(q_re).astype(BF16)
            qim_out[d, gg] = pad(-q_im).astype(BF16)

        for s in range(T):
            fwd = jnp.where(lane >= I * s, pltpu.roll(kalls[0], I * s, 1) if s else kalls[0], 0.0)
            sh = (I * (s + 1)) % W
            bwd = jnp.where(lane < I * (s + 1), pltpu.roll(kalls[1], sh, 1) if sh else kalls[1], 0.0)
            m_out[gg, s * I:(s + 1) * I, :] = (fwd + bwd).astype(BF16)

    srow = lax.broadcasted_iota(jnp.int32, (W, 1), 0) // I
    lane_p = lax.broadcasted_iota(jnp.int32, (W, 2 * P), 1)
    for d in range(2):
        dt = jnp.exp(ldtp_ref[d])
        abr, abi, cre, cim = _zoh(arp_ref[d], aip_ref[d], dt)
        bbr, bbi = _cmul(cre, cim, btpr_ref[d], btpi_ref[d])
        e_p = (T - 1) - srow if d == 0 else srow
        ppr, ppi = _cpow(abr, abi, e_p, (W, 2 * P))
        p_re, p_im = _cmul(ppr, ppi, jnp.concatenate([bbr] * T, axis=0), jnp.concatenate([bbi] * T, axis=0))
        for gg in range(2):
            own = (lane_p < P) if gg == 0 else (lane_p >= P)
            pre_out[d, gg] = jnp.where(own, p_re, 0.0).astype(BF16)
            pim_out[d, gg] = jnp.where(own, p_im, 0.0).astype(BF16)
        a16r, a16i = _cpow(abr, abi, jnp.full((1, 2 * P), T, jnp.int32), (1, 2 * P))
        a16r_out[d] = a16r
        a16i_out[d] = a16i


def _s5_prep(a_re, a_im, log_dt, b_re, b_im, c_re, c_im):
    _, G, P = a_re.shape
    I, T = SSM_GROUP, CHUNK
    W = T * I
    col = lambda a: a.reshape(2, G, P, 1)
    row = lambda a: a.reshape(2, G, 1, P)
    bt = lambda b: jnp.swapaxes(b, 2, 3)
    ct = lambda c: jnp.tile(jnp.swapaxes(c, 2, 3), (1, 1, 1, T))
    prow = lambda a: a.reshape(2, G // 2, 1, 2 * P)
    pbt = lambda b: (bt(b).reshape(2, G // 2, 2, I, P).transpose(0, 1, 3, 2, 4).reshape(2, G // 2, I, 2 * P))
    spec = lambda r, c: pl.BlockSpec((2, 2, r, c), lambda j: (0, j, 0, 0))
    pspec = lambda r, c: pl.BlockSpec((2, None, r, c), lambda j: (0, j, 0, 0))
    shp = lambda r, c, dt: jax.ShapeDtypeStruct((2, G, r, c), dt)
    return pl.pallas_call(
        _s5prep_kernel,
        grid=(G // 2,),
        in_specs=[spec(P, 1), spec(P, 1), spec(1, P), spec(1, P), spec(1, 1),
                  spec(I, P), spec(I, P), spec(P, W), spec(P, W),
                  pspec(1, 2 * P), pspec(1, 2 * P), pspec(1, 2 * P), pspec(I, 2 * P), pspec(I, 2 * P)],
        out_specs=[pl.BlockSpec((2, W, W), lambda j: (j, 0, 0)),
                   spec(W, 2 * P), spec(W, 2 * P), spec(2 * P, W), spec(2 * P, W),
                   pspec(1, 2 * P), pspec(1, 2 * P)],
        out_shape=[jax.ShapeDtypeStruct((G, W, W), BF16),
                   shp(W, 2 * P, BF16), shp(W, 2 * P, BF16), shp(2 * P, W, BF16), shp(2 * P, W, BF16),
                   jax.ShapeDtypeStruct((2, G // 2, 1, 2 * P), F32),
                   jax.ShapeDtypeStruct((2, G // 2, 1, 2 * P), F32)],
        compiler_params=_params(1),
        name="s5prep",
    )(col(a_re), col(a_im), row(a_re), row(a_im), log_dt.reshape(2, G, 1, 1),
      bt(b_re), bt(b_im), ct(c_re), ct(c_im),
      prow(a_re), prow(a_im), prow(jnp.repeat(log_dt, P, axis=-1)), pbt(b_re), pbt(b_im))


RELAY_ROWS = 16
S5_OUT_ROWS = 256


def _block_transpose(xs, masks):
    n = len(xs)
    rolled = []
    for delta in range(n):
        comp = xs[delta]
        for q in range(1, n):
            comp = jnp.where(masks[q], xs[(q + delta) % n], comp)
        rolled.append(pltpu.roll(comp, SSM_GROUP * delta, 1) if delta else comp)
    out = []
    for a in range(n):
        y = rolled[(-a) % n]
        for p in range(1, n):
            y = jnp.where(masks[p], rolled[(p - a) % n], y)
        out.append(y)
    return out


def _s5_kernel(*refs, batch, has_h0):
    if has_h0:
        (u_ref, m_ref, pre_ref, pim_ref, qre_ref, qim_ref, a16r_ref, a16i_ref, h0_ref,
         y_ref, fin_ref, z_scr, s_scr, yt_scr) = refs
    else:
        (u_ref, m_ref, pre_ref, pim_ref, qre_ref, qim_ref, a16r_ref, a16i_ref,
         y_ref, fin_ref, z_scr, s_scr, yt_scr) = refs
    I, T, RT, RC = SSM_GROUP, CHUNK, RELAY_ROWS, S5_OUT_ROWS
    gpb = LANES // I
    halves = T * I // LANES
    rows = u_ref.shape[0] // T
    n_chunks = rows // batch
    lane_blk = lax.broadcasted_iota(jnp.int32, (RT, LANES), 1) // I
    masks = [lane_blk == p for p in range(gpb)]

    def to_groups(i, carry):
        r0 = pl.multiple_of(i * RT, RT)
        for h in range(halves):
            xs = [u_ref[pl.ds(r0 * T + p + gpb * h, RT, stride=T), :] for p in range(gpb)]
            for g, zg in enumerate(_block_transpose(xs, masks)):
                z_scr[g, pl.ds(r0, RT), h * LANES:(h + 1) * LANES] = zg.astype(BF16)
        return carry

    lax.fori_loop(0, rows // RT, to_groups, 0)

    npair = gpb // 2
    for j in range(npair):
        for d in range(2):
            for c, pref in enumerate((pre_ref, pim_ref)):
                s_scr[j, d, c] = _dot(z_scr[2 * j], pref[d, 2 * j]) + _dot(z_scr[2 * j + 1], pref[d, 2 * j + 1])

    together = max(1, min(npair, 64 // batch // 2))
    for j0 in range(0, npair, together):
        chains = [(j, d) for j in range(j0, j0 + together) for d in range(2)]
        a = [(jnp.broadcast_to(a16r_ref[d, j], (batch, LANES)),
              jnp.broadcast_to(a16i_ref[d, j], (batch, LANES))) for j, d in chains]
        if has_h0:
            init = tuple(h0_ref[j, d, c] for j, d in chains for c in range(2))
        else:
            init = tuple(jnp.zeros((batch, LANES), F32) for _ in range(2 * len(chains)))

        def step(k, carry):
            out = []
            for i, (j, d) in enumerate(chains):
                hr, hi = carry[2 * i], carry[2 * i + 1]
                c = k if d == 0 else n_chunks - 1 - k
                rs = pl.ds(pl.multiple_of(c * batch, batch), batch)
                sr = s_scr[j, d, 0, rs, :]
                si = s_scr[j, d, 1, rs, :]
                s_scr[j, d, 0, rs, :] = hr
                s_scr[j, d, 1, rs, :] = hi
                ar, ai = a[i]
                out.append(ar * hr - ai * hi + sr)
                out.append(ar * hi + ai * hr + si)
            return tuple(out)

        fin = lax.fori_loop(0, n_chunks, step, init)
        for i, (j, d) in enumerate(chains):
            fin_ref[j, d, 0] = fin[2 * i]
            fin_ref[j, d, 1] = fin[2 * i + 1]

    def out_tile(i, carry):
        r0 = pl.multiple_of(i * RC, RC)
        rs = pl.ds(r0, RC)
        for g in range(gpb):
            j = g // 2
            y = _dot(z_scr[g, rs, :], m_ref[g])
            for d in range(2):
                y = (y + _dot(s_scr[j, d, 0, rs, :].astype(BF16), qre_ref[d, g])
                     + _dot(s_scr[j, d, 1, rs, :].astype(BF16), qim_ref[d, g]))
            yt_scr[g] = y

        def to_tokens(k, carry2):
            q0 = pl.multiple_of(k * RT, RT)
            for h in range(halves):
                xs = [yt_scr[g, pl.ds(q0, RT), h * LANES:(h + 1) * LANES] for g in range(gpb)]
                for p, yp in enumerate(_block_transpose(xs, masks)):
                    y_ref[pl.ds((r0 + q0) * T + p + gpb * h, RT, stride=T), :] = yp
            return carry2

        lax.fori_loop(0, RC // RT, to_tokens, 0)
        return carry

    lax.fori_loop(0, rows // RC, out_tile, 0)


def _s5_scan(u_blk, ops, h0):
    nb, nc, b, _, _ = u_blk.shape
    gpb = LANES // SSM_GROUP
    npair = gpb // 2
    W = CHUNK * SSM_GROUP
    P = SSM_STATE
    rows = nc * b
    m, pre, pim, qre, qim, a16r, a16i = ops
    spec = lambda r, c: pl.BlockSpec((2, gpb, r, c), lambda j: (0, j, 0, 0))
    pspec = pl.BlockSpec((2, npair, 1, 2 * P), lambda j: (0, j, 0, 0))
    act = lambda **kw: pl.BlockSpec((None, rows * CHUNK, LANES), lambda j: (j, 0, 0), **kw)
    st = pl.BlockSpec((None, npair, 2, 2, b, 2 * P), lambda j: (j, 0, 0, 0, 0, 0))
    in_specs = [act(pipeline_mode=pl.Buffered(1)), pl.BlockSpec((gpb, W, W), lambda j: (j, 0, 0)),
                spec(W, 2 * P), spec(W, 2 * P), spec(2 * P, W), spec(2 * P, W), pspec, pspec]
    args = [u_blk.reshape(nb, rows * CHUNK, LANES), m, pre, pim, qre, qim, a16r, a16i]
    if h0 is not None:
        in_specs.append(st)
        args.append(h0.reshape(b, 2, 2, nb, npair, 2, P).transpose(3, 4, 1, 2, 0, 5, 6)
                    .reshape(nb, npair, 2, 2, b, 2 * P))
    y, fin = pl.pallas_call(
        functools.partial(_s5_kernel, batch=b, has_h0=h0 is not None),
        grid=(nb,),
        in_specs=in_specs,
        out_specs=[act(), st],
        out_shape=[jax.ShapeDtypeStruct((nb, rows * CHUNK, LANES), F32),
                   jax.ShapeDtypeStruct((nb, npair, 2, 2, b, 2 * P), F32)],
        scratch_shapes=[pltpu.VMEM((gpb, rows, W), BF16),
                        pltpu.VMEM((npair, 2, 2, rows, 2 * P), F32),
                        pltpu.VMEM((gpb, S5_OUT_ROWS, W), F32)],
        compiler_params=_params(1),
        name="s5",
    )(*args)
    fin = (fin.reshape(nb, npair, 2, 2, b, 2, P).transpose(4, 2, 3, 0, 1, 5, 6)
           .reshape(b, 2, 2, nb * gpb, P))
    return y.reshape(u_blk.shape), fin


def _post_kernel(x_ref, attn_ref, y_ref, u_ref, d_ref, wglu_ref, bglu_ref, wout_ref, mod_ref, g2_ref,
                 wrt_ref, x1_out, h2_out, aff_out):
    m = mod_ref[...]
    gate1, shift2, scale2 = m[2:3], m[3:4], m[4:5]
    tm = x_ref.shape[0]
    tokens = lambda r: jnp.concatenate([r[gb].reshape(tm, LANES) for gb in range(r.shape[0])], axis=1)
    y = tokens(y_ref) + tokens(u_ref) * d_ref[...]
    y = jax.nn.gelu(y)
    y = y * jax.nn.sigmoid(_dot(y.astype(BF16), wglu_ref[...]) + bglu_ref[...])
    aw = attn_ref.shape[1]
    proj = _dot(attn_ref[...], wout_ref[:aw, :]) + _dot(y.astype(BF16), wout_ref[aw:, :])
    x1 = x_ref[...] + gate1 * proj
    x1_out[...] = x1
    h2 = (_rms_rows(x1, g2_ref[...]) * (1.0 + scale2) + shift2).astype(BF16)
    h2_out[...] = h2
    logits = _dot_nt(wrt_ref[...], h2)
    e = jnp.exp(logits - jnp.max(logits, axis=0, keepdims=True))
    aff_out[...] = e / jnp.sum(e, axis=0, keepdims=True)


def _post(x, attn, y_ssm, u, d_skip, wglu_bf, b_glu, wout_bf, mod, g2, wrt_bf):
    b, n, d = x.shape
    tm = min(n, 512)
    sw = u.shape[0] * LANES
    shared = mod.shape[0] == 1
    mod_idx = (lambda i, j: (0, 0, 0)) if shared else (lambda i, j: (i, 0, 0))
    const2 = lambda i, j: (0, 0)
    tok = lambda w: pl.BlockSpec((None, tm, w), lambda i, j: (i, j, 0))
    return pl.pallas_call(
        _post_kernel,
        grid=(b, n // tm),
        in_specs=[tok(d), tok(attn.shape[-1]), _chunked_spec(sw, tm), _chunked_spec(sw, tm),
                  pl.BlockSpec((1, sw), const2),
                  pl.BlockSpec(wglu_bf.shape, const2),
                  pl.BlockSpec((1, sw), const2),
                  pl.BlockSpec(wout_bf.shape, const2),
                  pl.BlockSpec((None, N_MOD, d), mod_idx),
                  pl.BlockSpec((1, d), const2),
                  pl.BlockSpec(wrt_bf.shape, const2)],
        out_specs=[tok(d), tok(d), pl.BlockSpec((None, N_EXPERTS, tm), lambda i, j: (i, 0, j))],
        out_shape=[jax.ShapeDtypeStruct((b, n, d), F32),
                   jax.ShapeDtypeStruct((b, n, d), BF16),
                   jax.ShapeDtypeStruct((b, N_EXPERTS, n), F32)],
        compiler_params=_params(2),
        name="post",
    )(x, attn, y_ssm, u, d_skip, wglu_bf, b_glu, wout_bf, mod, g2, wrt_bf)


def _route_kernel(aff_ref, pos_out, *, cap):
    a = aff_ref[...]
    rows, n = a.shape
    capf = jnp.float32(cap)

    def count(mask):
        return jnp.sum(jnp.where(mask, 1.0, 0.0), axis=1, keepdims=True)

    def search(i, bits):
        cand = bits | (jnp.int32(1) << (30 - i))
        ok = count(a >= lax.bitcast_convert_type(cand, F32)) >= capf
        return jnp.where(ok, cand, bits)

    thr_bits = lax.fori_loop(0, 31, search, jnp.zeros((rows, 1), jnp.int32))
    thr = lax.bitcast_convert_type(thr_bits, F32)
    gt = a > thr
    eq = a == thr
    before = (lax.broadcasted_iota(jnp.int32, (n, n), 0)
              < lax.broadcasted_iota(jnp.int32, (n, n), 1))
    tri = jnp.where(before, 1.0, 0.0).astype(BF16)
    eq_rank = _dot(jnp.where(eq, 1.0, 0.0).astype(BF16), tri)
    sel = gt | (eq & (eq_rank < capf - count(gt)))
    pos = _dot(jnp.where(sel, 1.0, 0.0).astype(BF16), tri)
    pos_out[...] = jnp.where(sel, pos, -1.0).astype(jnp.int32)


def _route(aff_t, cap):
    b, e, n = aff_t.shape
    rows = b * e
    pos = pl.pallas_call(
        functools.partial(_route_kernel, cap=cap),
        grid=(1,),
        in_specs=[pl.BlockSpec((rows, n), lambda i: (0, 0))],
        out_specs=pl.BlockSpec((rows, n), lambda i: (0, 0)),
        out_shape=jax.ShapeDtypeStruct((rows, n), jnp.int32),
        compiler_params=_params(1),
        name="route",
    )(aff_t.reshape(rows, n))
    return pos.reshape(b, e, n)


def _gather_kernel(h_ref, pos_ref, aff_ref, x_out, gate_out, p_scr, *, cap, group):
    n = h_ref.shape[0]
    slot = lax.broadcasted_iota(jnp.int32, (cap, n), 0)
    for e0 in range(0, N_EXPERTS, group):
        for j in range(group):
            e = e0 + j
            hit = pos_ref[e:e + 1, :] == slot
            p_scr[j * cap:(j + 1) * cap, :] = jnp.where(hit, 1.0, 0.0).astype(BF16)
            gate_out[e] = jnp.sum(jnp.where(hit, aff_ref[e:e + 1, :], 0.0), axis=1, keepdims=True)
        xs = _dot(p_scr[...], h_ref[...]).astype(BF16)
        for j in range(group):
            x_out[e0 + j] = xs[j * cap:(j + 1) * cap, :]


def _gather(h2, pos, aff_t, cap):
    b, n, d = h2.shape
    group = max(1, min(N_EXPERTS, MOE_ROWS // cap))
    return pl.pallas_call(
        functools.partial(_gather_kernel, cap=cap, group=group),
        grid=(b,),
        in_specs=[pl.BlockSpec((None, n, d), lambda i: (i, 0, 0)),
                  pl.BlockSpec((None, N_EXPERTS, n), lambda i: (i, 0, 0)),
                  pl.BlockSpec((None, N_EXPERTS, n), lambda i: (i, 0, 0))],
        out_specs=[pl.BlockSpec((None, N_EXPERTS, cap, d), lambda i: (i, 0, 0, 0)),
                   pl.BlockSpec((None, N_EXPERTS, cap, 1), lambda i: (i, 0, 0, 0))],
        out_shape=[jax.ShapeDtypeStruct((b, N_EXPERTS, cap, d), BF16),
                   jax.ShapeDtypeStruct((b, N_EXPERTS, cap, 1), F32)],
        scratch_shapes=[pltpu.VMEM((group * cap, n), BF16)],
        compiler_params=_params(1),
        name="gather",
    )(h2, pos, aff_t)


FFN_SPLIT = 4


def _ffn_kernel(x_ref, gate_ref, wg_ref, wu_ref, wd_ref, y_out):
    bt, cap, d = x_ref.shape
    fs = wg_ref.shape[1] // FFN_SPLIT
    x = x_ref[...].reshape(bt * cap, d)
    y = jnp.zeros((bt * cap, d), F32)
    for h in range(FFN_SPLIT):
        cols = slice(h * fs, (h + 1) * fs)
        g = _dot(x, wg_ref[:, cols].astype(BF16))
        hid = (g * jax.nn.sigmoid(g)) * _dot(x, wu_ref[:, cols].astype(BF16))
        y = y + _dot(hid.astype(BF16), wd_ref[cols, :].astype(BF16))
    y = y * gate_ref[...].reshape(bt * cap, 1)
    y_out[...] = y.astype(BF16).reshape(bt, cap, d)


def _ffn(xsel, gate, wg, wu, wd):
    b, e, cap, d = xsel.shape
    bt = max(1, min(b, MOE_ROWS // cap))
    f = wg.shape[-1]
    return pl.pallas_call(
        _ffn_kernel,
        grid=(e, b // bt),
        in_specs=[pl.BlockSpec((bt, None, cap, d), lambda i, j: (j, i, 0, 0)),
                  pl.BlockSpec((bt, None, cap, 1), lambda i, j: (j, i, 0, 0)),
                  pl.BlockSpec((None, d, f), lambda i, j: (i, 0, 0)),
                  pl.BlockSpec((None, d, f), lambda i, j: (i, 0, 0)),
                  pl.BlockSpec((None, f, d), lambda i, j: (i, 0, 0))],
        out_specs=pl.BlockSpec((bt, None, cap, d), lambda i, j: (j, i, 0, 0)),
        out_shape=jax.ShapeDtypeStruct((b, e, cap, d), BF16),
        compiler_params=_params(2),
        name="ffn",
    )(xsel, gate, wg, wu, wd)


def _combine_kernel(x1_ref, y_ref, post_ref, mod_ref, gf_ref, out_ref, *, cap):
    tn = x1_ref.shape[0]
    e, _, d = y_ref.shape
    width = min(e * cap, 512)
    per = width // cap
    pos = post_ref[...]
    slot = lax.broadcasted_iota(jnp.int32, (tn, width), 1) % cap
    if cap % LANES:
        post = pos.astype(F32).astype(BF16)
        lane = lax.broadcasted_iota(jnp.int32, (e, width), 1)
        erow = lax.broadcasted_iota(jnp.int32, (e, width), 0)
        slot = slot.astype(F32)
    acc = jnp.zeros((tn, d), F32)
    for c in range(e // per):
        if cap % LANES:
            expand = jnp.where(erow == c * per + lane // cap, 1.0, 0.0).astype(BF16)
            ids = _dot(post, expand)
        else:
            ids = jnp.concatenate([jnp.broadcast_to(pos[:, k:k + 1], (tn, cap))
                                   for k in range(c * per, (c + 1) * per)], axis=1)
        hit = ids == slot
        ys = y_ref[c * per:(c + 1) * per].reshape(width, d)
        acc = acc + _dot(jnp.where(hit, 1.0, 0.0).astype(BF16), ys)
    gate2 = mod_ref[...][5:6]
    out_ref[...] = _rms_rows(x1_ref[...] + gate2 * acc, gf_ref[...])


def _combine(x1, ysel, pos_t, mod, gf, cap):
    b, n, d = x1.shape
    tn = min(n, MOE_ROWS)
    shared = mod.shape[0] == 1
    mod_idx = (lambda i, j: (0, 0, 0)) if shared else (lambda i, j: (i, 0, 0))
    return pl.pallas_call(
        functools.partial(_combine_kernel, cap=cap),
        grid=(b, n // tn),
        in_specs=[pl.BlockSpec((None, tn, d), lambda i, j: (i, j, 0)),
                  pl.BlockSpec((None, N_EXPERTS, cap, d), lambda i, j: (i, 0, 0, 0)),
                  pl.BlockSpec((None, tn, N_EXPERTS), lambda i, j: (i, j, 0)),
                  pl.BlockSpec((None, N_MOD, d), mod_idx),
                  pl.BlockSpec((1, d), lambda i, j: (0, 0))],
        out_specs=pl.BlockSpec((None, tn, d), lambda i, j: (i, j, 0)),
        out_shape=jax.ShapeDtypeStruct((b, n, d), F32),
        compiler_params=_params(2),
        name="combine",
    )(x1, ysel, pos_t, mod, gf)


def _rope_tables(n_tokens):
    pairs = HEAD_DIM // 4
    rows = n_tokens // GRID_W
    row = jnp.repeat(jnp.arange(rows, dtype=F32), GRID_W)
    col = jnp.tile(jnp.arange(GRID_W, dtype=F32), rows)
    inv_freq = ROPE_THETA ** (-jnp.arange(pairs, dtype=F32) / pairs)
    ang = jnp.concatenate([row[:, None] * inv_freq, col[:, None] * inv_freq], axis=-1)
    reps = LANES // (HEAD_DIM // 2)
    return jnp.tile(jnp.cos(ang), (1, reps)), jnp.tile(jnp.sin(ang), (1, reps))


def _trunk(x, mod, w, rope_tabs, ctx_k, ctx_v, ctx_state):
    b, n, d = x.shape
    q, k, v, u = _inproj(x, mod, w["g1"], w["w_in"], w["qg"], w["kg"], w["bd"], rope_tabs)
    kv = [(k, v)]
    if ctx_k is not None:
        kv.append((ctx_k.reshape(b, -1, KV_WIDTH), ctx_v.reshape(b, -1, KV_WIDTH)))
    attn = _attention(q, kv)
    y_ssm, state = _s5_scan(u, w["s5ops"], ctx_state)
    x1, h2, aff_t = _post(x, attn, y_ssm, u, w["d_skip"], w["w_glu"], w["b_glu"], w["w_out"], mod,
                          w["g2"], w["w_router_t"])
    cap = CAPACITY_FACTOR * n // N_EXPERTS
    pos = _route(aff_t, cap)
    xsel, gate = _gather(h2, pos, aff_t, cap)
    ysel = _ffn(xsel, gate, w["w_gate"], w["w_up"], w["w_down"])
    out = _combine(x1, ysel, pos.transpose(0, 2, 1), mod, w["gf"], cap)
    return out, k, v, state


def kernel(x_prompt, x_sample, cache_k, cache_v, state_ssm, c, c_ctx, norm1_g, norm2_g, w_mod, b_mod, w_in, q_norm_g, k_norm_g, ssm_a_re, ssm_a_im, ssm_log_dt, ssm_b_re, ssm_b_im, ssm_c_re, ssm_c_im, ssm_d, w_glu, b_glu, w_out, w_router, w_gate, w_up, w_down, final_norm_g):
    assert norm1_g.shape[0] == 1, "single trunk layer"
    d = x_prompt.shape[-1]
    n_dec = c.shape[0]
    cond = jnp.concatenate([c_ctx[None, :], c, jnp.zeros((16 - 1 - n_dec, d), F32)], axis=0)
    mod = _modulation(cond, w_mod[0], b_mod[0]).reshape(16, N_MOD, d)
    head = jnp.arange(REP * HEAD_DIM) // HEAD_DIM
    w = {
        "g1": norm1_g, "g2": norm2_g, "gf": final_norm_g.reshape(1, d),
        "w_in": w_in[0].astype(BF16),
        "qg": jnp.tile(q_norm_g, (1, REP)),
        "kg": jnp.tile(k_norm_g, (1, REP)),
        "bd": (head[:, None] == head[None, :]).astype(BF16),
        "s5ops": _s5_prep(ssm_a_re[0], ssm_a_im[0], ssm_log_dt[0], ssm_b_re[0], ssm_b_im[0],
                          ssm_c_re[0], ssm_c_im[0]),
        "d_skip": ssm_d, "w_glu": w_glu[0].astype(BF16), "b_glu": b_glu,
        "w_out": w_out[0].astype(BF16),
        "w_router_t": w_router[0].T.astype(BF16),
        "w_gate": w_gate[0], "w_up": w_up[0], "w_down": w_down[0],
    }
    y_p, k_ctx, v_ctx, st_ctx = _trunk(x_prompt, mod[0:1], w, None, None, None, None)
    y_s, _, _, _ = _trunk(x_sample, mod[1:1 + n_dec], w, _rope_tables(x_sample.shape[1]),
                          cache_k[:, 0], cache_v[:, 0], state_ssm[:, 0])
    bp, np_ = x_prompt.shape[:2]
    new_k = k_ctx.reshape(bp, 1, np_, N_KV_HEADS, HEAD_DIM)
    new_v = v_ctx.reshape(bp, 1, np_, N_KV_HEADS, HEAD_DIM)
    return (y_p, y_s, new_k, new_v, st_ctx[:, None])
```

```python
import functools
import math

import jax
import jax.numpy as jnp
from jax import lax
from jax.experimental import pallas as pl
from jax.experimental.pallas import tpu as pltpu

F32 = jnp.float32
BF16 = jnp.bfloat16

HEAD_DIM = 64
N_HEADS = 8
N_KV_HEADS = 2
REP = N_HEADS // N_KV_HEADS
ATTN_WIDTH = N_HEADS * HEAD_DIM
KV_WIDTH = N_KV_HEADS * HEAD_DIM
SSM_GROUP = 16
SSM_STATE = 64
N_EXPERTS = 16
CAPACITY_FACTOR = 2
GRID_W = 64
ROPE_THETA = 10000.0
NORM_EPS = 1e-6
N_MOD = 6
CHUNK = 16
LANES = 128
VMEM_LIMIT = 56 * 1024 * 1024
MOE_ROWS = 1024


def _params(n_grid_axes, vmem=VMEM_LIMIT):
    return pltpu.CompilerParams(dimension_semantics=("arbitrary",) * n_grid_axes,
                                vmem_limit_bytes=vmem)


def _chunked_spec(width, tm):
    return pl.BlockSpec((width // LANES, tm // CHUNK, None, CHUNK, LANES), lambda i, j: (0, j, i, 0, 0))


def _dot(a, b):
    return jnp.dot(a, b, preferred_element_type=F32)


def _dot_nt(a, b):
    return lax.dot_general(a, b, (((1,), (1,)), ((), ())), preferred_element_type=F32)


def _rms_rows(x, g):
    return x * lax.rsqrt(jnp.mean(x * x, axis=-1, keepdims=True) + NORM_EPS) * g


def _mod_kernel(cond_ref, w_ref, b_ref, out_ref):
    c = cond_ref[...]
    s = (c * jax.nn.sigmoid(c)).astype(BF16)
    out_ref[...] = _dot(s, w_ref[...].astype(BF16)) + b_ref[...]


def _modulation(cond, w_mod, b_mod):
    rows, d = cond.shape
    n = w_mod.shape[1]
    tn = 1024
    return pl.pallas_call(
        _mod_kernel,
        grid=(n // tn,),
        in_specs=[pl.BlockSpec((rows, d), lambda j: (0, 0)),
                  pl.BlockSpec((d, tn), lambda j: (0, j)),
                  pl.BlockSpec((1, tn), lambda j: (0, j))],
        out_specs=pl.BlockSpec((rows, tn), lambda j: (0, j)),
        out_shape=jax.ShapeDtypeStruct((rows, n), F32),
        compiler_params=_params(1),
        name="mod",
    )(cond, w_mod, b_mod.reshape(1, n))


def _inproj_kernel(*refs, rope):
    if rope:
        (x_ref, mod_ref, g1_ref, win_ref, qg_ref, kg_ref, bd_ref, cos_ref, sin_ref,
         q_out, k_out, v_out, u_out) = refs
    else:
        (x_ref, mod_ref, g1_ref, win_ref, qg_ref, kg_ref, bd_ref,
         q_out, k_out, v_out, u_out) = refs
    x = x_ref[...]
    m = mod_ref[...]
    shift1, scale1 = m[0:1], m[1:2]
    h = _rms_rows(x, g1_ref[...]) * (1.0 + scale1) + shift1
    z = _dot(h.astype(BF16), win_ref[...])
    bd = bd_ref[...]
    if rope:
        cos = cos_ref[...]
        sin = sin_ref[...]
        lane = lax.broadcasted_iota(jnp.int32, cos.shape, 1)
        first_half = (lane % HEAD_DIM) < (HEAD_DIM // 2)

    def head_norm(zc, g):
        w = zc.shape[1]
        ss = _dot((zc * zc).astype(BF16), bd[:w, :w])
        return zc * lax.rsqrt(ss * (1.0 / HEAD_DIM) + NORM_EPS) * g[:, :w]

    def rope_lanes(zn):
        if not rope:
            return zn
        rot = jnp.where(first_half,
                        -pltpu.roll(zn, LANES - HEAD_DIM // 2, 1),
                        pltpu.roll(zn, HEAD_DIM // 2, 1))
        return zn * cos + rot * sin

    q_scale = HEAD_DIM ** -0.5 * math.log2(math.e)
    nw = bd.shape[0]
    for i in range(0, ATTN_WIDTH, nw):
        qn = head_norm(z[:, i:i + nw], qg_ref[...])
        for j in range(0, nw, LANES):
            q_out[:, i + j:i + j + LANES] = (rope_lanes(qn[:, j:j + LANES]) * q_scale).astype(BF16)
    k_out[...] = rope_lanes(head_norm(z[:, ATTN_WIDTH:ATTN_WIDTH + KV_WIDTH], kg_ref[...]))
    v_out[...] = z[:, ATTN_WIDTH + KV_WIDTH:ATTN_WIDTH + 2 * KV_WIDTH]
    u0 = ATTN_WIDTH + 2 * KV_WIDTH
    for gb in range(u_out.shape[0]):
        u_out[gb] = z[:, u0 + gb * LANES:u0 + (gb + 1) * LANES].reshape(u_out.shape[1:])


def _inproj(x, mod, g1, w_in_bf, qg, kg, bd, rope_tabs):
    b, n, d = x.shape
    tm = min(n, 512)
    ssm_w = w_in_bf.shape[1] - ATTN_WIDTH - 2 * KV_WIDTH
    shared = mod.shape[0] == 1
    mod_idx = (lambda i, j: (0, 0, 0)) if shared else (lambda i, j: (i, 0, 0))
    const2 = lambda i, j: (0, 0)
    in_specs = [pl.BlockSpec((None, tm, d), lambda i, j: (i, j, 0)),
                pl.BlockSpec((None, N_MOD, d), mod_idx),
                pl.BlockSpec((1, d), const2),
                pl.BlockSpec(w_in_bf.shape, const2),
                pl.BlockSpec(qg.shape, const2),
                pl.BlockSpec(kg.shape, const2),
                pl.BlockSpec(bd.shape, const2)]
    args = [x, mod, g1, w_in_bf, qg, kg, bd]
    if rope_tabs is not None:
        in_specs += [pl.BlockSpec((tm, LANES), lambda i, j: (j, 0))] * 2
        args += list(rope_tabs)
    tok = lambda w: pl.BlockSpec((None, tm, w), lambda i, j: (i, j, 0))
    return pl.pallas_call(
        functools.partial(_inproj_kernel, rope=rope_tabs is not None),
        grid=(b, n // tm),
        in_specs=in_specs,
        out_specs=[tok(ATTN_WIDTH), tok(KV_WIDTH), tok(KV_WIDTH), _chunked_spec(ssm_w, tm)],
        out_shape=[jax.ShapeDtypeStruct((b, n, ATTN_WIDTH), BF16),
                   jax.ShapeDtypeStruct((b, n, KV_WIDTH), F32),
                   jax.ShapeDtypeStruct((b, n, KV_WIDTH), F32),
                   jax.ShapeDtypeStruct((ssm_w // LANES, n // CHUNK, b, CHUNK, LANES), F32)],
        compiler_params=_params(2),
        name="inproj",
    )(*args)


KV_FILL_ROWS = 256
ATTN_Q_ROWS = 512
ATTN_KEY_SLAB = 256


def _attn_kernel(*refs, n_src):
    q_ref = refs[0]
    kv_refs = refs[1:1 + 2 * n_src]
    out_ref, k4_scr, vt_scr = refs[1 + 2 * n_src:]
    gw = REP * HEAD_DIM

    @pl.when(pl.program_id(1) == 0)
    def _():
        lane = lax.broadcasted_iota(jnp.int32, (KV_FILL_ROWS, KV_WIDTH), 1)
        first = lane < HEAD_DIM
        row0 = 0
        for k_src, v_src in zip(kv_refs[0::2], kv_refs[1::2]):
            for r in range(0, k_src.shape[0], KV_FILL_ROWS):
                rows = slice(row0 + r, row0 + r + KV_FILL_ROWS)
                x = k_src[r:r + KV_FILL_ROWS, :]
                sw = pltpu.roll(x, HEAD_DIM, 1)
                for g, xx in enumerate((jnp.where(first, x, sw), jnp.where(first, sw, x))):
                    xx = xx.astype(BF16)
                    k4_scr[g, rows, :] = jnp.concatenate([xx, xx], axis=1)
                vt = v_src[r:r + KV_FILL_ROWS, :].T
                for g in range(N_KV_HEADS):
                    vt_scr[g, :, rows] = vt[g * HEAD_DIM:(g + 1) * HEAD_DIM, :].astype(BF16)
            row0 += k_src.shape[0]

    lane_head = lax.broadcasted_iota(jnp.int32, (q_ref.shape[0], gw), 1) // HEAD_DIM

    n_keys = k4_scr.shape[1]
    slab = min(n_keys, ATTN_KEY_SLAB)

    def scores(h):
        g, r = divmod(h, REP)
        qg = q_ref[:, g * gw:(g + 1) * gw]
        qm = jnp.where(lane_head == r, qg, jnp.zeros_like(qg))
        sts = [_dot_nt(k4_scr[g, i:i + slab, :], qm) for i in range(0, n_keys, slab)]
        m = functools.reduce(jnp.maximum, [jnp.max(s, axis=0, keepdims=True) for s in sts])
        return sts, m

    heads = []
    nxt = scores(0)
    for h in range(N_HEADS):
        sts, m = nxt
        if h + 1 < N_HEADS:
            nxt = scores(h + 1)
        l = jnp.zeros_like(m)
        o = jnp.zeros((HEAD_DIM, q_ref.shape[0]), F32)
        for i, st in enumerate(sts):
            pt = jnp.exp2(st - m)
            l = l + jnp.sum(pt, axis=0, keepdims=True)
            o = o + _dot(vt_scr[h // REP, :, i * slab:(i + 1) * slab], pt.astype(BF16))
        heads.append(o / l)
    for i in range(0, N_HEADS, 2):
        pair = jnp.concatenate(heads[i:i + 2], axis=0)
        out_ref[:, i * HEAD_DIM:(i + 2) * HEAD_DIM] = pair.T.astype(BF16)


def _attention(q, kv_sources):
    b, n, _ = q.shape
    s = sum(k.shape[1] for k, _ in kv_sources)
    tq = min(n, ATTN_Q_ROWS)
    gw = REP * HEAD_DIM
    in_specs = [pl.BlockSpec((None, tq, ATTN_WIDTH), lambda i, j: (i, j, 0))]
    args = [q]
    for k, v in kv_sources:
        in_specs += [pl.BlockSpec((None, k.shape[1], KV_WIDTH), lambda i, j: (i, 0, 0))] * 2
        args += [k, v]
    return pl.pallas_call(
        functools.partial(_attn_kernel, n_src=len(kv_sources)),
        grid=(b, n // tq),
        in_specs=in_specs,
        out_specs=pl.BlockSpec((None, tq, ATTN_WIDTH), lambda i, j: (i, j, 0)),
        out_shape=jax.ShapeDtypeStruct((b, n, ATTN_WIDTH), BF16),
        scratch_shapes=[pltpu.VMEM((N_KV_HEADS, s, gw), BF16), pltpu.VMEM((N_KV_HEADS, HEAD_DIM, s), BF16)],
        compiler_params=_params(2),
        name="attn",
    )(*args)


def _cmul(ar, ai, br, bi):
    return ar * br - ai * bi, ar * bi + ai * br


def _cpow(ar, ai, e, shape):
    rr = jnp.ones(shape, F32)
    ri = jnp.zeros(shape, F32)
    pr = jnp.broadcast_to(ar, shape)
    pi = jnp.broadcast_to(ai, shape)
    e = jnp.broadcast_to(e, shape)
    for k in range(CHUNK.bit_length()):
        nr, ni = _cmul(rr, ri, pr, pi)
        bit = ((e >> k) & 1) == 1
        rr = jnp.where(bit, nr, rr)
        ri = jnp.where(bit, ni, ri)
        pr, pi = _cmul(pr, pi, pr, pi)
    return rr, ri


def _zoh(ar, ai, dt):
    mag = jnp.exp(ar * dt)
    abr = mag * jnp.cos(ai * dt)
    abi = mag * jnp.sin(ai * dt)
    den = ar * ar + ai * ai
    nre = abr - 1.0
    nim = abi
    cre = (nre * ar + nim * ai) / den
    cim = (nim * ar - nre * ai) / den
    return abr, abi, cre, cim


def _s5prep_kernel(arc_ref, aic_ref, arr_ref, air_ref, ldt_ref, btr_ref, bti_ref, ctr_ref, cti_ref,
                   arp_ref, aip_ref, ldtp_ref, btpr_ref, btpi_ref,
                   m_out, pre_out, pim_out, qre_out, qim_out, a16r_out, a16i_out):
    P, I, T = SSM_STATE, SSM_GROUP, CHUNK
    W = T * I
    hi = lax.Precision.HIGHEST
    blk = lax.broadcasted_iota(jnp.int32, (1, W), 1) // I
    lane = lax.broadcasted_iota(jnp.int32, (I, W), 1)
    zero_q = jnp.zeros((P, W), F32)
    for gg in range(2):
        kalls = []
        for d in range(2):
            dt = jnp.exp(ldt_ref[d, gg])
            abr_c, abi_c, _, _ = _zoh(arc_ref[d, gg], aic_ref[d, gg], dt)
            _, _, cre_r, cim_r = _zoh(arr_ref[d, gg], air_ref[d, gg], dt)
            bbr, bbi = _cmul(cre_r, cim_r, btr_ref[d, gg], bti_ref[d, gg])

            e_w = blk if d == 0 else (T - 1) - blk
            pwr, pwi = _cpow(abr_c, abi_c, e_w, (P, W))
            w_re, w_im = _cmul(pwr, pwi, ctr_ref[d, gg], cti_ref[d, gg])
            kalls.append(jnp.dot(bbr, w_re, precision=hi, preferred_element_type=F32)
                         - jnp.dot(bbi, w_im, precision=hi, preferred_element_type=F32))

            q_re, q_im = _cmul(w_re, w_im, jnp.broadcast_to(abr_c, (P, W)), jnp.broadcast_to(abi_c, (P, W)))
            pad = lambda q: jnp.concatenate([q, zero_q] if gg == 0 else [zero_q, q], axis=0)
            qre_out[d, gg] = pad(q_re).astype(BF16)
            qim_out[d, gg] = pad(-q_im).astype(BF16)

        for s in range(T):
            fwd = jnp.where(lane >= I * s, pltpu.roll(kalls[0], I * s, 1) if s else kalls[0], 0.0)
            sh = (I * (s + 1)) % W
            bwd = jnp.where(lane < I * (s + 1), pltpu.roll(kalls[1], sh, 1) if sh else kalls[1], 0.0)
            m_out[gg, s * I:(s + 1) * I, :] = (fwd + bwd).astype(BF16)

    srow = lax.broadcasted_iota(jnp.int32, (W, 1), 0) // I
    lane_p = lax.broadcasted_iota(jnp.int32, (W, 2 * P), 1)
    for d in range(2):
        dt = jnp.exp(ldtp_ref[d])
        abr, abi, cre, cim = _zoh(arp_ref[d], aip_ref[d], dt)
        bbr, bbi = _cmul(cre, cim, btpr_ref[d], btpi_ref[d])
        e_p = (T - 1) - srow if d == 0 else srow
        ppr, ppi = _cpow(abr, abi, e_p, (W, 2 * P))
        p_re, p_im = _cmul(ppr, ppi, jnp.concatenate([bbr] * T, axis=0), jnp.concatenate([bbi] * T, axis=0))
        for gg in range(2):
            own = (lane_p < P) if gg == 0 else (lane_p >= P)
            pre_out[d, gg] = jnp.where(own, p_re, 0.0).astype(BF16)
            pim_out[d, gg] = jnp.where(own, p_im, 0.0).astype(BF16)
        a16r, a16i = _cpow(abr, abi, jnp.full((1, 2 * P), T, jnp.int32), (1, 2 * P))
        a16r_out[d] = a16r
        a16i_out[d] = a16i


def _s5_prep(a_re, a_im, log_dt, b_re, b_im, c_re, c_im):
    _, G, P = a_re.shape
    I, T = SSM_GROUP, CHUNK
    W = T * I
    col = lambda a: a.reshape(2, G, P, 1)
    row = lambda a: a.reshape(2, G, 1, P)
    bt = lambda b: jnp.swapaxes(b, 2, 3)
    ct = lambda c: jnp.tile(jnp.swapaxes(c, 2, 3), (1, 1, 1, T))
    prow = lambda a: a.reshape(2, G // 2, 1, 2 * P)
    pbt = lambda b: (bt(b).reshape(2, G // 2, 2, I, P).transpose(0, 1, 3, 2, 4).reshape(2, G // 2, I, 2 * P))
    spec = lambda r, c: pl.BlockSpec((2, 2, r, c), lambda j: (0, j, 0, 0))
    pspec = lambda r, c: pl.BlockSpec((2, None, r, c), lambda j: (0, j, 0, 0))
    shp = lambda r, c, dt: jax.ShapeDtypeStruct((2, G, r, c), dt)
    return pl.pallas_call(
        _s5prep_kernel,
        grid=(G // 2,),
        in_specs=[spec(P, 1), spec(P, 1), spec(1, P), spec(1, P), spec(1, 1),
                  spec(I, P), spec(I, P), spec(P, W), spec(P, W),
                  pspec(1, 2 * P), pspec(1, 2 * P), pspec(1, 2 * P), pspec(I, 2 * P), pspec(I, 2 * P)],
        out_specs=[pl.BlockSpec((2, W, W), lambda j: (j, 0, 0)),
                   spec(W, 2 * P), spec(W, 2 * P), spec(2 * P, W), spec(2 * P, W),
                   pspec(1, 2 * P), pspec(1, 2 * P)],
        out_shape=[jax.ShapeDtypeStruct((G, W, W), BF16),
                   shp(W, 2 * P, BF16), shp(W, 2 * P, BF16), shp(2 * P, W, BF16), shp(2 * P, W, BF16),
                   jax.ShapeDtypeStruct((2, G // 2, 1, 2 * P), F32),
                   jax.ShapeDtypeStruct((2, G // 2, 1, 2 * P), F32)],
        compiler_params=_params(1),
        name="s5prep",
    )(col(a_re), col(a_im), row(a_re), row(a_im), log_dt.reshape(2, G, 1, 1),
      bt(b_re), bt(b_im), ct(c_re), ct(c_im),
      prow(a_re), prow(a_im), prow(jnp.repeat(log_dt, P, axis=-1)), pbt(b_re), pbt(b_im))


RELAY_ROWS = 16
S5_OUT_ROWS = 256


def _block_transpose(xs, masks):
    n = len(xs)
    rolled = []
    for delta in range(n):
        comp = xs[delta]
        for q in range(1, n):
            comp = jnp.where(masks[q], xs[(q + delta) % n], comp)
        rolled.append(pltpu.roll(comp, SSM_GROUP * delta, 1) if delta else comp)
    out = []
    for a in range(n):
        y = rolled[(-a) % n]
        for p in range(1, n):
            y = jnp.where(masks[p], rolled[(p - a) % n], y)
        out.append(y)
    return out


def _s5_kernel(*refs, batch, has_h0):
    if has_h0:
        (u_ref, m_ref, pre_ref, pim_ref, qre_ref, qim_ref, a16r_ref, a16i_ref, h0_ref,
         y_ref, fin_ref, z_scr, s_scr, yt_scr) = refs
    else:
        (u_ref, m_ref, pre_ref, pim_ref, qre_ref, qim_ref, a16r_ref, a16i_ref,
         y_ref, fin_ref, z_scr, s_scr, yt_scr) = refs
    I, T, RT, RC = SSM_GROUP, CHUNK, RELAY_ROWS, S5_OUT_ROWS
    gpb = LANES // I
    halves = T * I // LANES
    rows = u_ref.shape[0] // T
    n_chunks = rows // batch
    lane_blk = lax.broadcasted_iota(jnp.int32, (RT, LANES), 1) // I
    masks = [lane_blk == p for p in range(gpb)]

    def to_groups(i, carry):
        r0 = pl.multiple_of(i * RT, RT)
        for h in range(halves):
            xs = [u_ref[pl.ds(r0 * T + p + gpb * h, RT, stride=T), :] for p in range(gpb)]
            for g, zg in enumerate(_block_transpose(xs, masks)):
                z_scr[g, pl.ds(r0, RT), h * LANES:(h + 1) * LANES] = zg.astype(BF16)
        return carry

    lax.fori_loop(0, rows // RT, to_groups, 0)

    npair = gpb // 2
    for j in range(npair):
        for d in range(2):
            for c, pref in enumerate((pre_ref, pim_ref)):
                s_scr[j, d, c] = _dot(z_scr[2 * j], pref[d, 2 * j]) + _dot(z_scr[2 * j + 1], pref[d, 2 * j + 1])

    for j0 in range(npair):
        chains = [(j0, d) for d in range(2)]
        a = [(jnp.broadcast_to(a16r_ref[d, j], (batch, LANES)),
              jnp.broadcast_to(a16i_ref[d, j], (batch, LANES))) for j, d in chains]
        if has_h0:
            init = tuple(h0_ref[j, d, c] for j, d in chains for c in range(2))
        else:
            init = tuple(jnp.zeros((batch, LANES), F32) for _ in range(2 * len(chains)))

        def step(k, carry):
            out = []
            for i, (j, d) in enumerate(chains):
                hr, hi = carry[2 * i], carry[2 * i + 1]
                c = k if d == 0 else n_chunks - 1 - k
                rs = pl.ds(pl.multiple_of(c * batch, batch), batch)
                sr = s_scr[j, d, 0, rs, :]
                si = s_scr[j, d, 1, rs, :]
                s_scr[j, d, 0, rs, :] = hr
                s_scr[j, d, 1, rs, :] = hi
                ar, ai = a[i]
                out.append(ar * hr - ai * hi + sr)
                out.append(ar * hi + ai * hr + si)
            return tuple(out)

        fin = lax.fori_loop(0, n_chunks, step, init)
        for i, (j, d) in enumerate(chains):
            fin_ref[j, d, 0] = fin[2 * i]
            fin_ref[j, d, 1] = fin[2 * i + 1]

    def out_tile(i, carry):
        r0 = pl.multiple_of(i * RC, RC)
        rs = pl.ds(r0, RC)
        for g in range(gpb):
            j = g // 2
            y = _dot(z_scr[g, rs, :], m_ref[g])
            for d in range(2):
                y = (y + _dot(s_scr[j, d, 0, rs, :].astype(BF16), qre_ref[d, g])
                     + _dot(s_scr[j, d, 1, rs, :].astype(BF16), qim_ref[d, g]))
            yt_scr[g] = y

        def to_tokens(k, carry2):
            q0 = pl.multiple_of(k * RT, RT)
            for h in range(halves):
                xs = [yt_scr[g, pl.ds(q0, RT), h * LANES:(h + 1) * LANES] for g in range(gpb)]
                for p, yp in enumerate(_block_transpose(xs, masks)):
                    y_ref[pl.ds((r0 + q0) * T + p + gpb * h, RT, stride=T), :] = yp
            return carry2

        lax.fori_loop(0, RC // RT, to_tokens, 0)
        return carry

    lax.fori_loop(0, rows // RC, out_tile, 0)


def _s5_scan(u_blk, ops, h0):
    nb, nc, b, _, _ = u_blk.shape
    gpb = LANES // SSM_GROUP
    npair = gpb // 2
    W = CHUNK * SSM_GROUP
    P = SSM_STATE
    rows = nc * b
    m, pre, pim, qre, qim, a16r, a16i = ops
    spec = lambda r, c: pl.BlockSpec((2, gpb, r, c), lambda j: (0, j, 0, 0))
    pspec = pl.BlockSpec((2, npair, 1, 2 * P), lambda j: (0, j, 0, 0))
    act = lambda **kw: pl.BlockSpec((None, rows * CHUNK, LANES), lambda j: (j, 0, 0), **kw)
    st = pl.BlockSpec((None, npair, 2, 2, b, 2 * P), lambda j: (j, 0, 0, 0, 0, 0))
    in_specs = [act(pipeline_mode=pl.Buffered(1)), pl.BlockSpec((gpb, W, W), lambda j: (j, 0, 0)),
                spec(W, 2 * P), spec(W, 2 * P), spec(2 * P, W), spec(2 * P, W), pspec, pspec]
    args = [u_blk.reshape(nb, rows * CHUNK, LANES), m, pre, pim, qre, qim, a16r, a16i]
    if h0 is not None:
        in_specs.append(st)
        args.append(h0.reshape(b, 2, 2, nb, npair, 2, P).transpose(3, 4, 1, 2, 0, 5, 6)
                    .reshape(nb, npair, 2, 2, b, 2 * P))
    y, fin = pl.pallas_call(
        functools.partial(_s5_kernel, batch=b, has_h0=h0 is not None),
        grid=(nb,),
        in_specs=in_specs,
        out_specs=[act(), st],
        out_shape=[jax.ShapeDtypeStruct((nb, rows * CHUNK, LANES), F32),
                   jax.ShapeDtypeStruct((nb, npair, 2, 2, b, 2 * P), F32)],
        scratch_shapes=[pltpu.VMEM((gpb, rows, W), BF16),
                        pltpu.VMEM((npair, 2, 2, rows, 2 * P), F32),
                        pltpu.VMEM((gpb, S5_OUT_ROWS, W), F32)],
        compiler_params=_params(1),
        name="s5",
    )(*args)
    fin = (fin.reshape(nb, npair, 2, 2, b, 2, P).transpose(4, 2, 3, 0, 1, 5, 6)
           .reshape(b, 2, 2, nb * gpb, P))
    return y.reshape(u_blk.shape), fin


def _post_kernel(x_ref, attn_ref, y_ref, u_ref, d_ref, wglu_ref, bglu_ref, wout_ref, mod_ref, g2_ref,
                 wrt_ref, x1_out, h2_out, aff_out):
    m = mod_ref[...]
    gate1, shift2, scale2 = m[2:3], m[3:4], m[4:5]
    tm = x_ref.shape[0]
    tokens = lambda r: jnp.concatenate([r[gb].reshape(tm, LANES) for gb in range(r.shape[0])], axis=1)
    y = tokens(y_ref) + tokens(u_ref) * d_ref[...]
    y = jax.nn.gelu(y)
    y = y * jax.nn.sigmoid(_dot(y.astype(BF16), wglu_ref[...]) + bglu_ref[...])
    aw = attn_ref.shape[1]
    proj = _dot(attn_ref[...], wout_ref[:aw, :]) + _dot(y.astype(BF16), wout_ref[aw:, :])
    x1 = x_ref[...] + gate1 * proj
    x1_out[...] = x1
    h2 = (_rms_rows(x1, g2_ref[...]) * (1.0 + scale2) + shift2).astype(BF16)
    h2_out[...] = h2
    logits = _dot_nt(wrt_ref[...], h2)
    e = jnp.exp(logits - jnp.max(logits, axis=0, keepdims=True))
    aff_out[...] = e / jnp.sum(e, axis=0, keepdims=True)


def _post(x, attn, y_ssm, u, d_skip, wglu_bf, b_glu, wout_bf, mod, g2, wrt_bf):
    b, n, d = x.shape
    tm = min(n, 512)
    sw = u.shape[0] * LANES
    shared = mod.shape[0] == 1
    mod_idx = (lambda i, j: (0, 0, 0)) if shared else (lambda i, j: (i, 0, 0))
    const2 = lambda i, j: (0, 0)
    tok = lambda w: pl.BlockSpec((None, tm, w), lambda i, j: (i, j, 0))
    return pl.pallas_call(
        _post_kernel,
        grid=(b, n // tm),
        in_specs=[tok(d), tok(attn.shape[-1]), _chunked_spec(sw, tm), _chunked_spec(sw, tm),
                  pl.BlockSpec((1, sw), const2),
                  pl.BlockSpec(wglu_bf.shape, const2),
                  pl.BlockSpec((1, sw), const2),
                  pl.BlockSpec(wout_bf.shape, const2),
                  pl.BlockSpec((None, N_MOD, d), mod_idx),
                  pl.BlockSpec((1, d), const2),
                  pl.BlockSpec(wrt_bf.shape, const2)],
        out_specs=[tok(d), tok(d), pl.BlockSpec((None, N_EXPERTS, tm), lambda i, j: (i, 0, j))],
        out_shape=[jax.ShapeDtypeStruct((b, n, d), F32),
                   jax.ShapeDtypeStruct((b, n, d), BF16),
                   jax.ShapeDtypeStruct((b, N_EXPERTS, n), F32)],
        compiler_params=_params(2),
        name="post",
    )(x, attn, y_ssm, u, d_skip, wglu_bf, b_glu, wout_bf, mod, g2, wrt_bf)


MOE_TOKEN_TILE = 256


def _route_kernel(aff_ref, pos_out, off_out, *, cap):
    a = aff_ref[...]
    rows, n = a.shape
    capf = jnp.float32(cap)

    def count(mask):
        return jnp.sum(jnp.where(mask, 1.0, 0.0), axis=1, keepdims=True)

    def search(i, bits):
        cand = bits | (jnp.int32(1) << (30 - i))
        ok = count(a >= lax.bitcast_convert_type(cand, F32)) >= capf
        return jnp.where(ok, cand, bits)

    thr_bits = lax.fori_loop(0, 31, search, jnp.zeros((rows, 1), jnp.int32))
    thr = lax.bitcast_convert_type(thr_bits, F32)
    gt = a > thr
    eq = a == thr
    before = (lax.broadcasted_iota(jnp.int32, (n, n), 0)
              < lax.broadcasted_iota(jnp.int32, (n, n), 1))
    tri = jnp.where(before, 1.0, 0.0).astype(BF16)
    eq_rank = _dot(jnp.where(eq, 1.0, 0.0).astype(BF16), tri)
    sel = gt | (eq & (eq_rank < capf - count(gt)))
    sel_bf = jnp.where(sel, 1.0, 0.0).astype(BF16)
    pos = _dot(sel_bf, tri)
    pos_out[...] = jnp.where(sel, pos, -1.0).astype(jnp.int32)
    tile_start = lax.broadcasted_iota(jnp.int32, (n, LANES), 1) * MOE_TOKEN_TILE
    in_front = lax.broadcasted_iota(jnp.int32, (n, LANES), 0) < tile_start
    off_out[...] = _dot(sel_bf, jnp.where(in_front, 1.0, 0.0).astype(BF16)).astype(jnp.int32)


def _route(aff_t, cap):
    b, e, n = aff_t.shape
    rows = b * e
    nt = n // MOE_TOKEN_TILE
    pos, off = pl.pallas_call(
        functools.partial(_route_kernel, cap=cap),
        grid=(1,),
        in_specs=[pl.BlockSpec((rows, n), lambda i: (0, 0))],
        out_specs=[pl.BlockSpec((rows, n), lambda i: (0, 0)), pl.BlockSpec((rows, LANES), lambda i: (0, 0))],
        out_shape=[jax.ShapeDtypeStruct((rows, n), jnp.int32), jax.ShapeDtypeStruct((rows, LANES), jnp.int32)],
        compiler_params=_params(1),
        name="route",
    )(aff_t.reshape(rows, n))
    return pos.reshape(b, e, n), off[:, :nt + 1].reshape(-1)


SLOT_ALIGN = 16


def _window_rows(cap, n):
    mean = cap * MOE_TOKEN_TILE // n
    return min(cap, max(SLOT_ALIGN, -(-2 * mean // SLOT_ALIGN) * SLOT_ALIGN))


def _tile_windows(off_ref, request, tile, n_tiles, wn):
    starts, rounds = [], jnp.int32(0)
    for e in range(N_EXPERTS):
        i = (request * N_EXPERTS + e) * (n_tiles + 1) + tile
        start = (off_ref[i] // SLOT_ALIGN) * SLOT_ALIGN
        starts.append(start)
        rounds = jnp.maximum(rounds, (off_ref[i + 1] - start + wn - 1) // wn)
    return starts, rounds


def _window(start, r, wn, cap):
    lo = start + r * wn
    base = pl.multiple_of(jnp.minimum(lo, cap - wn), SLOT_ALIGN)
    return lo, base


def _gather_kernel(off_ref, h_ref, pos_ref, aff_ref, x_out, gate_out, p_scr, *, cap, wn):
    n = h_ref.shape[0]
    tt = MOE_TOKEN_TILE
    x_out[...] = jnp.zeros_like(x_out)
    gate_out[...] = jnp.zeros_like(gate_out)
    w_col = lax.broadcasted_iota(jnp.int32, (wn, 1), 0)
    for kt in range(n // tt):
        tile = slice(kt * tt, (kt + 1) * tt)
        starts, rounds = _tile_windows(off_ref, pl.program_id(0), kt, n // tt, wn)

        def one_round(r, carry):
            for e in range(N_EXPERTS):
                lo, base = _window(starts[e], r, wn, cap)
                sid = base + w_col
                sid = jnp.where((sid >= lo) & (sid < lo + wn), sid, -2)
                hit = pos_ref[e:e + 1, tile] == sid
                p_scr[e * wn:(e + 1) * wn, :] = jnp.where(hit, 1.0, 0.0).astype(BF16)
                gate_out[e, pl.ds(base, wn), :] += jnp.sum(jnp.where(hit, aff_ref[e:e + 1, tile], 0.0),
                                                           axis=1, keepdims=True)
            xs = _dot(p_scr[...], h_ref[tile, :]).astype(BF16)
            for e in range(N_EXPERTS):
                _, base = _window(starts[e], r, wn, cap)
                x_out[e, pl.ds(base, wn), :] += xs[e * wn:(e + 1) * wn, :]
            return carry

        lax.fori_loop(0, rounds, one_round, 0)


def _gather(h2, pos, aff_t, off, cap):
    b, n, d = h2.shape
    wn = _window_rows(cap, n)
    grid_spec = pltpu.PrefetchScalarGridSpec(
        num_scalar_prefetch=1,
        grid=(b,),
        in_specs=[pl.BlockSpec((None, n, d), lambda i, off: (i, 0, 0)),
                  pl.BlockSpec((None, N_EXPERTS, n), lambda i, off: (i, 0, 0)),
                  pl.BlockSpec((None, N_EXPERTS, n), lambda i, off: (i, 0, 0))],
        out_specs=[pl.BlockSpec((None, N_EXPERTS, cap, d), lambda i, off: (i, 0, 0, 0)),
                   pl.BlockSpec((None, N_EXPERTS, cap, 1), lambda i, off: (i, 0, 0, 0))],
        scratch_shapes=[pltpu.VMEM((N_EXPERTS * wn, MOE_TOKEN_TILE), BF16)])
    return pl.pallas_call(
        functools.partial(_gather_kernel, cap=cap, wn=wn),
        grid_spec=grid_spec,
        out_shape=[jax.ShapeDtypeStruct((b, N_EXPERTS, cap, d), BF16),
                   jax.ShapeDtypeStruct((b, N_EXPERTS, cap, 1), F32)],
        compiler_params=_params(1),
        name="gather",
    )(off, h2, pos, aff_t)


FFN_SPLIT = 4


def _ffn_kernel(x_ref, gate_ref, wg_ref, wu_ref, wd_ref, y_out):
    bt, cap, d = x_ref.shape
    fs = wg_ref.shape[1] // FFN_SPLIT
    x = x_ref[...].reshape(bt * cap, d)
    y = jnp.zeros((bt * cap, d), F32)
    for h in range(FFN_SPLIT):
        cols = slice(h * fs, (h + 1) * fs)
        g = _dot(x, wg_ref[:, cols].astype(BF16))
        hid = (g * jax.nn.sigmoid(g)) * _dot(x, wu_ref[:, cols].astype(BF16))
        y = y + _dot(hid.astype(BF16), wd_ref[cols, :].astype(BF16))
    y = y * gate_ref[...].reshape(bt * cap, 1)
    y_out[...] = y.astype(BF16).reshape(bt, cap, d)


def _ffn(xsel, gate, wg, wu, wd):
    b, e, cap, d = xsel.shape
    bt = max(1, min(b, MOE_ROWS // cap))
    f = wg.shape[-1]
    return pl.pallas_call(
        _ffn_kernel,
        grid=(e, b // bt),
        in_specs=[pl.BlockSpec((bt, None, cap, d), lambda i, j: (j, i, 0, 0)),
                  pl.BlockSpec((bt, None, cap, 1), lambda i, j: (j, i, 0, 0)),
                  pl.BlockSpec((None, d, f), lambda i, j: (i, 0, 0)),
                  pl.BlockSpec((None, d, f), lambda i, j: (i, 0, 0)),
                  pl.BlockSpec((None, f, d), lambda i, j: (i, 0, 0))],
        out_specs=pl.BlockSpec((bt, None, cap, d), lambda i, j: (j, i, 0, 0)),
        out_shape=jax.ShapeDtypeStruct((b, e, cap, d), BF16),
        compiler_params=_params(2),
        name="ffn",
    )(xsel, gate, wg, wu, wd)


def _combine_kernel(off_ref, x1_ref, y_ref, post_ref, mod_ref, gf_ref, out_ref, ycat_scr, *, cap, wn, n_tiles):
    tn, d = x1_ref.shape
    width = N_EXPERTS * wn
    post = post_ref[...].astype(F32).astype(BF16)
    lane = lax.broadcasted_iota(jnp.int32, (N_EXPERTS, width), 1)
    erow = lax.broadcasted_iota(jnp.int32, (N_EXPERTS, width), 0)
    ids = _dot(post, jnp.where(erow == lane // wn, 1.0, 0.0).astype(BF16))
    lane1 = lax.broadcasted_iota(jnp.int32, (1, width), 1)
    starts, rounds = _tile_windows(off_ref, pl.program_id(0), pl.program_id(1), n_tiles, wn)

    def one_round(r, acc):
        sid = jnp.full((1, width), -2, jnp.int32)
        for e in range(N_EXPERTS):
            lo, base = _window(starts[e], r, wn, cap)
            ycat_scr[e * wn:(e + 1) * wn, :] = y_ref[e, pl.ds(base, wn), :]
            s = base + (lane1 - e * wn)
            sid = jnp.where((lane1 // wn == e) & (s >= lo) & (s < lo + wn), s, sid)
        hit = ids == sid.astype(F32)
        return acc + _dot(jnp.where(hit, 1.0, 0.0).astype(BF16), ycat_scr[...])

    acc = lax.fori_loop(0, rounds, one_round, jnp.zeros((tn, d), F32))
    gate2 = mod_ref[...][5:6]
    out_ref[...] = _rms_rows(x1_ref[...] + gate2 * acc, gf_ref[...])


def _combine(x1, ysel, pos_t, off, mod, gf, cap):
    b, n, d = x1.shape
    tn = MOE_TOKEN_TILE
    wn = _window_rows(cap, n)
    shared = mod.shape[0] == 1
    mod_idx = (lambda i, j, off: (0, 0, 0)) if shared else (lambda i, j, off: (i, 0, 0))
    grid_spec = pltpu.PrefetchScalarGridSpec(
        num_scalar_prefetch=1,
        grid=(b, n // tn),
        in_specs=[pl.BlockSpec((None, tn, d), lambda i, j, off: (i, j, 0)),
                  pl.BlockSpec((None, N_EXPERTS, cap, d), lambda i, j, off: (i, 0, 0, 0)),
                  pl.BlockSpec((None, tn, N_EXPERTS), lambda i, j, off: (i, j, 0)),
                  pl.BlockSpec((None, N_MOD, d), mod_idx),
                  pl.BlockSpec((1, d), lambda i, j, off: (0, 0))],
        out_specs=pl.BlockSpec((None, tn, d), lambda i, j, off: (i, j, 0)),
        scratch_shapes=[pltpu.VMEM((N_EXPERTS * wn, d), BF16)])
    return pl.pallas_call(
        functools.partial(_combine_kernel, cap=cap, wn=wn, n_tiles=n // tn),
        grid_spec=grid_spec,
        out_shape=jax.ShapeDtypeStruct((b, n, d), F32),
        compiler_params=_params(2),
        name="combine",
    )(off, x1, ysel, pos_t, mod, gf)


def _rope_tables(n_tokens):
    pairs = HEAD_DIM // 4
    rows = n_tokens // GRID_W
    row = jnp.repeat(jnp.arange(rows, dtype=F32), GRID_W)
    col = jnp.tile(jnp.arange(GRID_W, dtype=F32), rows)
    inv_freq = ROPE_THETA ** (-jnp.arange(pairs, dtype=F32) / pairs)
    ang = jnp.concatenate([row[:, None] * inv_freq, col[:, None] * inv_freq], axis=-1)
    reps = LANES // (HEAD_DIM // 2)
    return jnp.tile(jnp.cos(ang), (1, reps)), jnp.tile(jnp.sin(ang), (1, reps))


def _trunk(x, mod, w, rope_tabs, ctx_k, ctx_v, ctx_state):
    b, n, d = x.shape
    q, k, v, u = _inproj(x, mod, w["g1"], w["w_in"], w["qg"], w["kg"], w["bd"], rope_tabs)
    kv = [(k, v)]
    if ctx_k is not None:
        kv.append((ctx_k.reshape(b, -1, KV_WIDTH), ctx_v.reshape(b, -1, KV_WIDTH)))
    attn = _attention(q, kv)
    y_ssm, state = _s5_scan(u, w["s5ops"], ctx_state)
    x1, h2, aff_t = _post(x, attn, y_ssm, u, w["d_skip"], w["w_glu"], w["b_glu"], w["w_out"], mod,
                          w["g2"], w["w_router_t"])
    cap = CAPACITY_FACTOR * n // N_EXPERTS
    pos, off = _route(aff_t, cap)
    xsel, gate = _gather(h2, pos, aff_t, off, cap)
    ysel = _ffn(xsel, gate, w["w_gate"], w["w_up"], w["w_down"])
    out = _combine(x1, ysel, pos.transpose(0, 2, 1), off, mod, w["gf"], cap)
    return out, k, v, state


def kernel(x_prompt, x_sample, cache_k, cache_v, state_ssm, c, c_ctx, norm1_g, norm2_g, w_mod, b_mod, w_in, q_norm_g, k_norm_g, ssm_a_re, ssm_a_im, ssm_log_dt, ssm_b_re, ssm_b_im, ssm_c_re, ssm_c_im, ssm_d, w_glu, b_glu, w_out, w_router, w_gate, w_up, w_down, final_norm_g):
    assert norm1_g.shape[0] == 1, "single trunk layer"
    d = x_prompt.shape[-1]
    n_dec = c.shape[0]
    cond = jnp.concatenate([c_ctx[None, :], c, jnp.zeros((16 - 1 - n_dec, d), F32)], axis=0)
    mod = _modulation(cond, w_mod[0], b_mod[0]).reshape(16, N_MOD, d)
    head = jnp.arange(REP * HEAD_DIM) // HEAD_DIM
    w = {
        "g1": norm1_g, "g2": norm2_g, "gf": final_norm_g.reshape(1, d),
        "w_in": w_in[0].astype(BF16),
        "qg": jnp.tile(q_norm_g, (1, REP)),
        "kg": jnp.tile(k_norm_g, (1, REP)),
        "bd": (head[:, None] == head[None, :]).astype(BF16),
        "s5ops": _s5_prep(ssm_a_re[0], ssm_a_im[0], ssm_log_dt[0], ssm_b_re[0], ssm_b_im[0],
                          ssm_c_re[0], ssm_c_im[0]),
        "d_skip": ssm_d, "w_glu": w_glu[0].astype(BF16), "b_glu": b_glu,
        "w_out": w_out[0].astype(BF16),
        "w_router_t": w_router[0].T.astype(BF16),
        "w_gate": w_gate[0], "w_up": w_up[0], "w_down": w_down[0],
    }
    y_p, k_ctx, v_ctx, st_ctx = _trunk(x_prompt, mod[0:1], w, None, None, None, None)
    y_s, _, _, _ = _trunk(x_sample, mod[1:1 + n_dec], w, _rope_tables(x_sample.shape[1]),
                          cache_k[:, 0], cache_v[:, 0], state_ssm[:, 0])
    bp, np_ = x_prompt.shape[:2]
    new_k = k_ctx.reshape(bp, 1, np_, N_KV_HEADS, HEAD_DIM)
    new_v = v_ctx.reshape(bp, 1, np_, N_KV_HEADS, HEAD_DIM)
    return (y_p, y_s, new_k, new_v, st_ctx[:, None])
```

```python
import functools
import math

import jax
import jax.numpy as jnp
from jax import lax
from jax.experimental import pallas as pl
from jax.experimental.pallas import tpu as pltpu

F32 = jnp.float32
BF16 = jnp.bfloat16

HEAD_DIM = 64
N_HEADS = 8
N_KV_HEADS = 2
REP = N_HEADS // N_KV_HEADS
ATTN_WIDTH = N_HEADS * HEAD_DIM
KV_WIDTH = N_KV_HEADS * HEAD_DIM
SSM_GROUP = 16
SSM_STATE = 64
N_EXPERTS = 16
CAPACITY_FACTOR = 2
GRID_W = 64
ROPE_THETA = 10000.0
NORM_EPS = 1e-6
N_MOD = 6
CHUNK = 16
LANES = 128
VMEM_LIMIT = 56 * 1024 * 1024
MOE_ROWS = 1024
TOKEN_ROWS = 1024


def _params(n_grid_axes, vmem=VMEM_LIMIT):
    return pltpu.CompilerParams(dimension_semantics=("arbitrary",) * n_grid_axes,
                                vmem_limit_bytes=vmem)


def _chunked_spec(width, tm):
    return pl.BlockSpec((width // LANES, tm // CHUNK, None, CHUNK, LANES), lambda i, j: (0, j, i, 0, 0))


def _dot(a, b):
    return jnp.dot(a, b, preferred_element_type=F32)


def _dot_nt(a, b):
    return lax.dot_general(a, b, (((1,), (1,)), ((), ())), preferred_element_type=F32)


def _rms_rows(x, g):
    return x * lax.rsqrt(jnp.mean(x * x, axis=-1, keepdims=True) + NORM_EPS) * g


def _mod_kernel(cond_ref, w_ref, b_ref, out_ref):
    c = cond_ref[...]
    s = (c * jax.nn.sigmoid(c)).astype(BF16)
    out_ref[...] = _dot(s, w_ref[...].astype(BF16)) + b_ref[...]


def _modulation(cond, w_mod, b_mod):
    rows, d = cond.shape
    n = w_mod.shape[1]
    tn = 1024
    return pl.pallas_call(
        _mod_kernel,
        grid=(n // tn,),
        in_specs=[pl.BlockSpec((rows, d), lambda j: (0, 0)),
                  pl.BlockSpec((d, tn), lambda j: (0, j)),
                  pl.BlockSpec((1, tn), lambda j: (0, j))],
        out_specs=pl.BlockSpec((rows, tn), lambda j: (0, j)),
        out_shape=jax.ShapeDtypeStruct((rows, n), F32),
        compiler_params=_params(1),
        name="mod",
    )(cond, w_mod, b_mod.reshape(1, n))


def _inproj_kernel(*refs, rope):
    if rope:
        (x_ref, mod_ref, g1_ref, win_ref, qg_ref, kg_ref, bd_ref, cos_ref, sin_ref,
         q_out, k_out, v_out, u_out) = refs
    else:
        (x_ref, mod_ref, g1_ref, win_ref, qg_ref, kg_ref, bd_ref,
         q_out, k_out, v_out, u_out) = refs
    x = x_ref[...]
    m = mod_ref[...]
    shift1, scale1 = m[0:1], m[1:2]
    h = _rms_rows(x, g1_ref[...]) * (1.0 + scale1) + shift1
    z = _dot(h.astype(BF16), win_ref[...])
    bd = bd_ref[...]
    if rope:
        cos = cos_ref[...]
        sin = sin_ref[...]
        lane = lax.broadcasted_iota(jnp.int32, cos.shape, 1)
        first_half = (lane % HEAD_DIM) < (HEAD_DIM // 2)

    def head_norm(zc, g):
        w = zc.shape[1]
        ss = _dot((zc * zc).astype(BF16), bd[:w, :w])
        return zc * lax.rsqrt(ss * (1.0 / HEAD_DIM) + NORM_EPS) * g[:, :w]

    def rope_lanes(zn):
        if not rope:
            return zn
        rot = jnp.where(first_half,
                        -pltpu.roll(zn, LANES - HEAD_DIM // 2, 1),
                        pltpu.roll(zn, HEAD_DIM // 2, 1))
        return zn * cos + rot * sin

    q_scale = HEAD_DIM ** -0.5 * math.log2(math.e)
    nw = bd.shape[0]
    for i in range(0, ATTN_WIDTH, nw):
        qn = head_norm(z[:, i:i + nw], qg_ref[...])
        for j in range(0, nw, LANES):
            q_out[:, i + j:i + j + LANES] = (rope_lanes(qn[:, j:j + LANES]) * q_scale).astype(BF16)
    k_out[...] = rope_lanes(head_norm(z[:, ATTN_WIDTH:ATTN_WIDTH + KV_WIDTH], kg_ref[...]))
    v_out[...] = z[:, ATTN_WIDTH + KV_WIDTH:ATTN_WIDTH + 2 * KV_WIDTH]
    u0 = ATTN_WIDTH + 2 * KV_WIDTH
    for gb in range(u_out.shape[0]):
        u_out[gb] = z[:, u0 + gb * LANES:u0 + (gb + 1) * LANES].reshape(u_out.shape[1:])


def _inproj(x, mod, g1, w_in_bf, qg, kg, bd, rope_tabs):
    b, n, d = x.shape
    tm = min(n, TOKEN_ROWS)
    ssm_w = w_in_bf.shape[1] - ATTN_WIDTH - 2 * KV_WIDTH
    shared = mod.shape[0] == 1
    mod_idx = (lambda i, j: (0, 0, 0)) if shared else (lambda i, j: (i, 0, 0))
    const2 = lambda i, j: (0, 0)
    in_specs = [pl.BlockSpec((None, tm, d), lambda i, j: (i, j, 0)),
                pl.BlockSpec((None, N_MOD, d), mod_idx),
                pl.BlockSpec((1, d), const2),
                pl.BlockSpec(w_in_bf.shape, const2),
                pl.BlockSpec(qg.shape, const2),
                pl.BlockSpec(kg.shape, const2),
                pl.BlockSpec(bd.shape, const2)]
    args = [x, mod, g1, w_in_bf, qg, kg, bd]
    if rope_tabs is not None:
        in_specs += [pl.BlockSpec((tm, LANES), lambda i, j: (j, 0))] * 2
        args += list(rope_tabs)
    tok = lambda w: pl.BlockSpec((None, tm, w), lambda i, j: (i, j, 0))
    return pl.pallas_call(
        functools.partial(_inproj_kernel, rope=rope_tabs is not None),
        grid=(b, n // tm),
        in_specs=in_specs,
        out_specs=[tok(ATTN_WIDTH), tok(KV_WIDTH), tok(KV_WIDTH), _chunked_spec(ssm_w, tm)],
        out_shape=[jax.ShapeDtypeStruct((b, n, ATTN_WIDTH), BF16),
                   jax.ShapeDtypeStruct((b, n, KV_WIDTH), F32),
                   jax.ShapeDtypeStruct((b, n, KV_WIDTH), F32),
                   jax.ShapeDtypeStruct((ssm_w // LANES, n // CHUNK, b, CHUNK, LANES), F32)],
        compiler_params=_params(2),
        name="inproj",
    )(*args)


KV_FILL_ROWS = 256
ATTN_Q_ROWS = 512
ATTN_KEY_SLAB = 256


def _attn_kernel(*refs, n_src):
    q_ref = refs[0]
    kv_refs = refs[1:1 + 2 * n_src]
    out_ref, k4_scr, vt_scr = refs[1 + 2 * n_src:]
    gw = REP * HEAD_DIM

    @pl.when(pl.program_id(1) == 0)
    def _():
        lane = lax.broadcasted_iota(jnp.int32, (KV_FILL_ROWS, KV_WIDTH), 1)
        first = lane < HEAD_DIM
        row0 = 0
        for k_src, v_src in zip(kv_refs[0::2], kv_refs[1::2]):
            for r in range(0, k_src.shape[0], KV_FILL_ROWS):
                rows = slice(row0 + r, row0 + r + KV_FILL_ROWS)
                x = k_src[r:r + KV_FILL_ROWS, :]
                sw = pltpu.roll(x, HEAD_DIM, 1)
                for g, xx in enumerate((jnp.where(first, x, sw), jnp.where(first, sw, x))):
                    xx = xx.astype(BF16)
                    k4_scr[g, rows, :] = jnp.concatenate([xx, xx], axis=1)
                vt = v_src[r:r + KV_FILL_ROWS, :].T
                for g in range(N_KV_HEADS):
                    vt_scr[g, :, rows] = vt[g * HEAD_DIM:(g + 1) * HEAD_DIM, :].astype(BF16)
            row0 += k_src.shape[0]

    lane_head = lax.broadcasted_iota(jnp.int32, (q_ref.shape[0], gw), 1) // HEAD_DIM

    n_keys = k4_scr.shape[1]
    slab = min(n_keys, ATTN_KEY_SLAB)

    def scores(h):
        g, r = divmod(h, REP)
        qg = q_ref[:, g * gw:(g + 1) * gw]
        qm = jnp.where(lane_head == r, qg, jnp.zeros_like(qg))
        sts = [_dot_nt(k4_scr[g, i:i + slab, :], qm) for i in range(0, n_keys, slab)]
        m = functools.reduce(jnp.maximum, [jnp.max(s, axis=0, keepdims=True) for s in sts])
        return sts, m

    heads = []
    nxt = scores(0)
    for h in range(N_HEADS):
        sts, m = nxt
        if h + 1 < N_HEADS:
            nxt = scores(h + 1)
        l = jnp.zeros_like(m)
        o = jnp.zeros((HEAD_DIM, q_ref.shape[0]), F32)
        for i, st in enumerate(sts):
            pt = jnp.exp2(st - m)
            l = l + jnp.sum(pt, axis=0, keepdims=True)
            o = o + _dot(vt_scr[h // REP, :, i * slab:(i + 1) * slab], pt.astype(BF16))
        heads.append(o / l)
    for i in range(0, N_HEADS, 2):
        pair = jnp.concatenate(heads[i:i + 2], axis=0)
        out_ref[:, i * HEAD_DIM:(i + 2) * HEAD_DIM] = pair.T.astype(BF16)


def _attention(q, kv_sources):
    b, n, _ = q.shape
    s = sum(k.shape[1] for k, _ in kv_sources)
    tq = min(n, ATTN_Q_ROWS)
    gw = REP * HEAD_DIM
    in_specs = [pl.BlockSpec((None, tq, ATTN_WIDTH), lambda i, j: (i, j, 0))]
    args = [q]
    for k, v in kv_sources:
        in_specs += [pl.BlockSpec((None, k.shape[1], KV_WIDTH), lambda i, j: (i, 0, 0))] * 2
        args += [k, v]
    return pl.pallas_call(
        functools.partial(_attn_kernel, n_src=len(kv_sources)),
        grid=(b, n // tq),
        in_specs=in_specs,
        out_specs=pl.BlockSpec((None, tq, ATTN_WIDTH), lambda i, j: (i, j, 0)),
        out_shape=jax.ShapeDtypeStruct((b, n, ATTN_WIDTH), BF16),
        scratch_shapes=[pltpu.VMEM((N_KV_HEADS, s, gw), BF16), pltpu.VMEM((N_KV_HEADS, HEAD_DIM, s), BF16)],
        compiler_params=_params(2),
        name="attn",
    )(*args)


def _cmul(ar, ai, br, bi):
    return ar * br - ai * bi, ar * bi + ai * br


def _cpow(ar, ai, e, shape):
    rr = jnp.ones(shape, F32)
    ri = jnp.zeros(shape, F32)
    pr = jnp.broadcast_to(ar, shape)
    pi = jnp.broadcast_to(ai, shape)
    e = jnp.broadcast_to(e, shape)
    for k in range(CHUNK.bit_length()):
        nr, ni = _cmul(rr, ri, pr, pi)
        bit = ((e >> k) & 1) == 1
        rr = jnp.where(bit, nr, rr)
        ri = jnp.where(bit, ni, ri)
        pr, pi = _cmul(pr, pi, pr, pi)
    return rr, ri


def _zoh(ar, ai, dt):
    mag = jnp.exp(ar * dt)
    abr = mag * jnp.cos(ai * dt)
    abi = mag * jnp.sin(ai * dt)
    den = ar * ar + ai * ai
    nre = abr - 1.0
    nim = abi
    cre = (nre * ar + nim * ai) / den
    cim = (nim * ar - nre * ai) / den
    return abr, abi, cre, cim


def _s5prep_kernel(arc_ref, aic_ref, arr_ref, air_ref, ldt_ref, btr_ref, bti_ref, ctr_ref, cti_ref,
                   arp_ref, aip_ref, ldtp_ref, btpr_ref, btpi_ref,
                   m_out, p_out, q_out, a16r_out, a16i_out):
    P, I, T = SSM_STATE, SSM_GROUP, CHUNK
    W = T * I
    hi = lax.Precision.HIGHEST
    blk = lax.broadcasted_iota(jnp.int32, (1, W), 1) // I
    lane = lax.broadcasted_iota(jnp.int32, (I, W), 1)
    zero_q = jnp.zeros((P, W), F32)
    for gg in range(2):
        kalls = []
        for d in range(2):
            dt = jnp.exp(ldt_ref[d, gg])
            abr_c, abi_c, _, _ = _zoh(arc_ref[d, gg], aic_ref[d, gg], dt)
            _, _, cre_r, cim_r = _zoh(arr_ref[d, gg], air_ref[d, gg], dt)
            bbr, bbi = _cmul(cre_r, cim_r, btr_ref[d, gg], bti_ref[d, gg])

            e_w = blk if d == 0 else (T - 1) - blk
            pwr, pwi = _cpow(abr_c, abi_c, e_w, (P, W))
            w_re, w_im = _cmul(pwr, pwi, ctr_ref[d, gg], cti_ref[d, gg])
            kalls.append(jnp.dot(bbr, w_re, precision=hi, preferred_element_type=F32)
                         - jnp.dot(bbi, w_im, precision=hi, preferred_element_type=F32))

            q_re, q_im = _cmul(w_re, w_im, jnp.broadcast_to(abr_c, (P, W)), jnp.broadcast_to(abi_c, (P, W)))
            pad = lambda q: jnp.concatenate([q, zero_q] if gg == 0 else [zero_q, q], axis=0)
            q_out[gg, (2 * d) * 2 * P:(2 * d + 1) * 2 * P, :] = pad(q_re).astype(BF16)
            q_out[gg, (2 * d + 1) * 2 * P:(2 * d + 2) * 2 * P, :] = pad(-q_im).astype(BF16)

        for s in range(T):
            fwd = jnp.where(lane >= I * s, pltpu.roll(kalls[0], I * s, 1) if s else kalls[0], 0.0)
            sh = (I * (s + 1)) % W
            bwd = jnp.where(lane < I * (s + 1), pltpu.roll(kalls[1], sh, 1) if sh else kalls[1], 0.0)
            m_out[gg, s * I:(s + 1) * I, :] = (fwd + bwd).astype(BF16)

    srow = lax.broadcasted_iota(jnp.int32, (W, 1), 0) // I
    lane_p = lax.broadcasted_iota(jnp.int32, (W, 2 * P), 1)
    for d in range(2):
        dt = jnp.exp(ldtp_ref[d])
        abr, abi, cre, cim = _zoh(arp_ref[d], aip_ref[d], dt)
        bbr, bbi = _cmul(cre, cim, btpr_ref[d], btpi_ref[d])
        e_p = (T - 1) - srow if d == 0 else srow
        ppr, ppi = _cpow(abr, abi, e_p, (W, 2 * P))
        p_re, p_im = _cmul(ppr, ppi, jnp.concatenate([bbr] * T, axis=0), jnp.concatenate([bbi] * T, axis=0))
        for gg in range(2):
            own = (lane_p < P) if gg == 0 else (lane_p >= P)
            p_out[gg, :, (2 * d) * 2 * P:(2 * d + 1) * 2 * P] = jnp.where(own, p_re, 0.0).astype(BF16)
            p_out[gg, :, (2 * d + 1) * 2 * P:(2 * d + 2) * 2 * P] = jnp.where(own, p_im, 0.0).astype(BF16)
        a16r, a16i = _cpow(abr, abi, jnp.full((1, 2 * P), T, jnp.int32), (1, 2 * P))
        a16r_out[d] = a16r
        a16i_out[d] = a16i


def _s5_prep(a_re, a_im, log_dt, b_re, b_im, c_re, c_im):
    _, G, P = a_re.shape
    I, T = SSM_GROUP, CHUNK
    W = T * I
    col = lambda a: a.reshape(2, G, P, 1)
    row = lambda a: a.reshape(2, G, 1, P)
    bt = lambda b: jnp.swapaxes(b, 2, 3)
    ct = lambda c: jnp.tile(jnp.swapaxes(c, 2, 3), (1, 1, 1, T))
    prow = lambda a: a.reshape(2, G // 2, 1, 2 * P)
    pbt = lambda b: (bt(b).reshape(2, G // 2, 2, I, P).transpose(0, 1, 3, 2, 4).reshape(2, G // 2, I, 2 * P))
    spec = lambda r, c: pl.BlockSpec((2, 2, r, c), lambda j: (0, j, 0, 0))
    pspec = lambda r, c: pl.BlockSpec((2, None, r, c), lambda j: (0, j, 0, 0))
    return pl.pallas_call(
        _s5prep_kernel,
        grid=(G // 2,),
        in_specs=[spec(P, 1), spec(P, 1), spec(1, P), spec(1, P), spec(1, 1),
                  spec(I, P), spec(I, P), spec(P, W), spec(P, W),
                  pspec(1, 2 * P), pspec(1, 2 * P), pspec(1, 2 * P), pspec(I, 2 * P), pspec(I, 2 * P)],
        out_specs=[pl.BlockSpec((2, W, W), lambda j: (j, 0, 0)),
                   pl.BlockSpec((2, W, 8 * P), lambda j: (j, 0, 0)),
                   pl.BlockSpec((2, 8 * P, W), lambda j: (j, 0, 0)),
                   pspec(1, 2 * P), pspec(1, 2 * P)],
        out_shape=[jax.ShapeDtypeStruct((G, W, W), BF16),
                   jax.ShapeDtypeStruct((G, W, 8 * P), BF16),
                   jax.ShapeDtypeStruct((G, 8 * P, W), BF16),
                   jax.ShapeDtypeStruct((2, G // 2, 1, 2 * P), F32),
                   jax.ShapeDtypeStruct((2, G // 2, 1, 2 * P), F32)],
        compiler_params=_params(1),
        name="s5prep",
    )(col(a_re), col(a_im), row(a_re), row(a_im), log_dt.reshape(2, G, 1, 1),
      bt(b_re), bt(b_im), ct(c_re), ct(c_im),
      prow(a_re), prow(a_im), prow(jnp.repeat(log_dt, P, axis=-1)), pbt(b_re), pbt(b_im))


RELAY_ROWS = 16
S5_OUT_ROWS = 256


def _block_transpose(xs, masks):
    n = len(xs)
    rolled = []
    for delta in range(n):
        comp = xs[delta]
        for q in range(1, n):
            comp = jnp.where(masks[q], xs[(q + delta) % n], comp)
        rolled.append(pltpu.roll(comp, SSM_GROUP * delta, 1) if delta else comp)
    out = []
    for a in range(n):
        y = rolled[(-a) % n]
        for p in range(1, n):
            y = jnp.where(masks[p], rolled[(p - a) % n], y)
        out.append(y)
    return out


def _s5_kernel(*refs, batch, has_h0):
    if has_h0:
        (u_ref, m_ref, p_ref, q_ref, a16r_ref, a16i_ref, h0_ref,
         y_ref, fin_ref, z_scr, s_scr, yt_scr) = refs
    else:
        (u_ref, m_ref, p_ref, q_ref, a16r_ref, a16i_ref,
         y_ref, fin_ref, z_scr, s_scr, yt_scr) = refs
    I, T, RT, RC = SSM_GROUP, CHUNK, RELAY_ROWS, S5_OUT_ROWS
    gpb = LANES // I
    halves = T * I // LANES
    rows = u_ref.shape[0] // T
    n_chunks = rows // batch
    lane_blk = lax.broadcasted_iota(jnp.int32, (RT, LANES), 1) // I
    masks = [lane_blk == p for p in range(gpb)]

    def to_groups(i, carry):
        r0 = pl.multiple_of(i * RT, RT)
        for h in range(halves):
            xs = [u_ref[pl.ds(r0 * T + p + gpb * h, RT, stride=T), :] for p in range(gpb)]
            for g, zg in enumerate(_block_transpose(xs, masks)):
                z_scr[g, pl.ds(r0, RT), h * LANES:(h + 1) * LANES] = zg.astype(BF16)
        return carry

    lax.fori_loop(0, rows // RT, to_groups, 0)

    npair = gpb // 2
    for j in range(npair):
        s_scr[j] = _dot(z_scr[2 * j], p_ref[2 * j]) + _dot(z_scr[2 * j + 1], p_ref[2 * j + 1])
    part = lambda d, c: slice((2 * d + c) * LANES, (2 * d + c + 1) * LANES)

    for j0 in range(npair):
        chains = [(j0, d) for d in range(2)]
        a = [(jnp.broadcast_to(a16r_ref[d, j], (batch, LANES)),
              jnp.broadcast_to(a16i_ref[d, j], (batch, LANES))) for j, d in chains]
        if has_h0:
            init = tuple(h0_ref[j, d, c] for j, d in chains for c in range(2))
        else:
            init = tuple(jnp.zeros((batch, LANES), F32) for _ in range(2 * len(chains)))

        def step(k, carry):
            out = []
            for i, (j, d) in enumerate(chains):
                hr, hi = carry[2 * i], carry[2 * i + 1]
                c = k if d == 0 else n_chunks - 1 - k
                rs = pl.ds(pl.multiple_of(c * batch, batch), batch)
                sr = s_scr[j, rs, part(d, 0)]
                si = s_scr[j, rs, part(d, 1)]
                s_scr[j, rs, part(d, 0)] = hr
                s_scr[j, rs, part(d, 1)] = hi
                ar, ai = a[i]
                out.append(ar * hr - ai * hi + sr)
                out.append(ar * hi + ai * hr + si)
            return tuple(out)

        fin = lax.fori_loop(0, n_chunks, step, init)
        for i, (j, d) in enumerate(chains):
            fin_ref[j, d, 0] = fin[2 * i]
            fin_ref[j, d, 1] = fin[2 * i + 1]

    def out_tile(i, carry):
        r0 = pl.multiple_of(i * RC, RC)
        rs = pl.ds(r0, RC)
        for g in range(gpb):
            yt_scr[g] = _dot(z_scr[g, rs, :], m_ref[g]) + _dot(s_scr[g // 2, rs, :].astype(BF16), q_ref[g])

        def to_tokens(k, carry2):
            q0 = pl.multiple_of(k * RT, RT)
            for h in range(halves):
                xs = [yt_scr[g, pl.ds(q0, RT), h * LANES:(h + 1) * LANES] for g in range(gpb)]
                for p, yp in enumerate(_block_transpose(xs, masks)):
                    y_ref[pl.ds((r0 + q0) * T + p + gpb * h, RT, stride=T), :] = yp
            return carry2

        lax.fori_loop(0, RC // RT, to_tokens, 0)
        return carry

    lax.fori_loop(0, rows // RC, out_tile, 0)


def _s5_scan(u_blk, ops, h0):
    nb, nc, b, _, _ = u_blk.shape
    gpb = LANES // SSM_GROUP
    npair = gpb // 2
    W = CHUNK * SSM_GROUP
    P = SSM_STATE
    rows = nc * b
    m, p_op, q_op, a16r, a16i = ops
    pspec = pl.BlockSpec((2, npair, 1, 2 * P), lambda j: (0, j, 0, 0))
    act = lambda **kw: pl.BlockSpec((None, rows * CHUNK, LANES), lambda j: (j, 0, 0), **kw)
    st = pl.BlockSpec((None, npair, 2, 2, b, 2 * P), lambda j: (j, 0, 0, 0, 0, 0))
    in_specs = [act(pipeline_mode=pl.Buffered(1)), pl.BlockSpec((gpb, W, W), lambda j: (j, 0, 0)),
                pl.BlockSpec((gpb, W, 8 * P), lambda j: (j, 0, 0)),
                pl.BlockSpec((gpb, 8 * P, W), lambda j: (j, 0, 0)), pspec, pspec]
    args = [u_blk.reshape(nb, rows * CHUNK, LANES), m, p_op, q_op, a16r, a16i]
    if h0 is not None:
        in_specs.append(st)
        args.append(h0.reshape(b, 2, 2, nb, npair, 2, P).transpose(3, 4, 1, 2, 0, 5, 6)
                    .reshape(nb, npair, 2, 2, b, 2 * P))
    y, fin = pl.pallas_call(
        functools.partial(_s5_kernel, batch=b, has_h0=h0 is not None),
        grid=(nb,),
        in_specs=in_specs,
        out_specs=[act(), st],
        out_shape=[jax.ShapeDtypeStruct((nb, rows * CHUNK, LANES), F32),
                   jax.ShapeDtypeStruct((nb, npair, 2, 2, b, 2 * P), F32)],
        scratch_shapes=[pltpu.VMEM((gpb, rows, W), BF16),
                        pltpu.VMEM((npair, rows, 8 * P), F32),
                        pltpu.VMEM((gpb, S5_OUT_ROWS, W), F32)],
        compiler_params=_params(1),
        name="s5",
    )(*args)
    fin = (fin.reshape(nb, npair, 2, 2, b, 2, P).transpose(4, 2, 3, 0, 1, 5, 6)
           .reshape(b, 2, 2, nb * gpb, P))
    return y.reshape(u_blk.shape), fin


def _post_kernel(x_ref, attn_ref, y_ref, u_ref, d_ref, wglu_ref, bglu_ref, wout_ref, mod_ref, g2_ref,
                 wrt_ref, x1_out, h2_out, aff_out):
    m = mod_ref[...]
    gate1, shift2, scale2 = m[2:3], m[3:4], m[4:5]
    tm = x_ref.shape[0]
    tokens = lambda r: jnp.concatenate([r[gb].reshape(tm, LANES) for gb in range(r.shape[0])], axis=1)
    y = tokens(y_ref) + tokens(u_ref) * d_ref[...]
    y = jax.nn.gelu(y)
    y = y * jax.nn.sigmoid(_dot(y.astype(BF16), wglu_ref[...]) + bglu_ref[...])
    aw = attn_ref.shape[1]
    proj = _dot(attn_ref[...], wout_ref[:aw, :]) + _dot(y.astype(BF16), wout_ref[aw:, :])
    x1 = x_ref[...] + gate1 * proj
    x1_out[...] = x1
    h2 = (_rms_rows(x1, g2_ref[...]) * (1.0 + scale2) + shift2).astype(BF16)
    h2_out[...] = h2
    logits = _dot_nt(wrt_ref[...], h2)
    e = jnp.exp(logits - jnp.max(logits, axis=0, keepdims=True))
    aff_out[...] = e / jnp.sum(e, axis=0, keepdims=True)


def _post(x, attn, y_ssm, u, d_skip, wglu_bf, b_glu, wout_bf, mod, g2, wrt_bf):
    b, n, d = x.shape
    tm = min(n, TOKEN_ROWS)
    sw = u.shape[0] * LANES
    shared = mod.shape[0] == 1
    mod_idx = (lambda i, j: (0, 0, 0)) if shared else (lambda i, j: (i, 0, 0))
    const2 = lambda i, j: (0, 0)
    tok = lambda w: pl.BlockSpec((None, tm, w), lambda i, j: (i, j, 0))
    return pl.pallas_call(
        _post_kernel,
        grid=(b, n // tm),
        in_specs=[tok(d), tok(attn.shape[-1]), _chunked_spec(sw, tm), _chunked_spec(sw, tm),
                  pl.BlockSpec((1, sw), const2),
                  pl.BlockSpec(wglu_bf.shape, const2),
                  pl.BlockSpec((1, sw), const2),
                  pl.BlockSpec(wout_bf.shape, const2),
                  pl.BlockSpec((None, N_MOD, d), mod_idx),
                  pl.BlockSpec((1, d), const2),
                  pl.BlockSpec(wrt_bf.shape, const2)],
        out_specs=[tok(d), tok(d), pl.BlockSpec((None, N_EXPERTS, tm), lambda i, j: (i, 0, j))],
        out_shape=[jax.ShapeDtypeStruct((b, n, d), F32),
                   jax.ShapeDtypeStruct((b, n, d), BF16),
                   jax.ShapeDtypeStruct((b, N_EXPERTS, n), F32)],
        compiler_params=_params(2),
        name="post",
    )(x, attn, y_ssm, u, d_skip, wglu_bf, b_glu, wout_bf, mod, g2, wrt_bf)


MOE_TOKEN_TILE = 256


def _route_kernel(aff_ref, pos_out, off_out, *, cap):
    a = aff_ref[...]
    rows, n = a.shape
    capf = jnp.float32(cap)

    def count(mask):
        return jnp.sum(jnp.where(mask, 1.0, 0.0), axis=1, keepdims=True)

    def search(i, bits):
        cand = bits | (jnp.int32(1) << (30 - i))
        ok = count(a >= lax.bitcast_convert_type(cand, F32)) >= capf
        return jnp.where(ok, cand, bits)

    thr_bits = lax.fori_loop(0, 31, search, jnp.zeros((rows, 1), jnp.int32))
    thr = lax.bitcast_convert_type(thr_bits, F32)
    gt = a > thr
    eq = a == thr
    before = (lax.broadcasted_iota(jnp.int32, (n, n), 0)
              < lax.broadcasted_iota(jnp.int32, (n, n), 1))
    tri = jnp.where(before, 1.0, 0.0).astype(BF16)
    eq_rank = _dot(jnp.where(eq, 1.0, 0.0).astype(BF16), tri)
    sel = gt | (eq & (eq_rank < capf - count(gt)))
    sel_bf = jnp.where(sel, 1.0, 0.0).astype(BF16)
    pos = _dot(sel_bf, tri)
    pos_out[...] = jnp.where(sel, pos, -1.0).astype(jnp.int32)
    tile_start = lax.broadcasted_iota(jnp.int32, (n, LANES), 1) * MOE_TOKEN_TILE
    in_front = lax.broadcasted_iota(jnp.int32, (n, LANES), 0) < tile_start
    off_out[...] = _dot(sel_bf, jnp.where(in_front, 1.0, 0.0).astype(BF16)).astype(jnp.int32)


def _route(aff_t, cap):
    b, e, n = aff_t.shape
    rows = b * e
    nt = n // MOE_TOKEN_TILE
    pos, off = pl.pallas_call(
        functools.partial(_route_kernel, cap=cap),
        grid=(1,),
        in_specs=[pl.BlockSpec((rows, n), lambda i: (0, 0))],
        out_specs=[pl.BlockSpec((rows, n), lambda i: (0, 0)), pl.BlockSpec((rows, LANES), lambda i: (0, 0))],
        out_shape=[jax.ShapeDtypeStruct((rows, n), jnp.int32), jax.ShapeDtypeStruct((rows, LANES), jnp.int32)],
        compiler_params=_params(1),
        name="route",
    )(aff_t.reshape(rows, n))
    return pos.reshape(b, e, n), off[:, :nt + 1].reshape(-1)


SLOT_ALIGN = 16


def _window_rows(cap, n):
    mean = cap * MOE_TOKEN_TILE // n
    return min(cap, max(SLOT_ALIGN, -(-2 * mean // SLOT_ALIGN) * SLOT_ALIGN))


def _tile_windows(off_ref, request, tile, n_tiles, wn):
    starts, rounds = [], jnp.int32(0)
    for e in range(N_EXPERTS):
        i = (request * N_EXPERTS + e) * (n_tiles + 1) + tile
        start = (off_ref[i] // SLOT_ALIGN) * SLOT_ALIGN
        starts.append(start)
        rounds = jnp.maximum(rounds, (off_ref[i + 1] - start + wn - 1) // wn)
    return starts, rounds


def _window(start, r, wn, cap):
    lo = start + r * wn
    base = pl.multiple_of(jnp.minimum(lo, cap - wn), SLOT_ALIGN)
    return lo, base


def _gather_kernel(off_ref, h_ref, pos_ref, aff_ref, x_out, gate_out, p_scr, *, cap, wn):
    n = h_ref.shape[0]
    tt = MOE_TOKEN_TILE
    x_out[...] = jnp.zeros_like(x_out)
    gate_out[...] = jnp.zeros_like(gate_out)
    w_col = lax.broadcasted_iota(jnp.int32, (wn, 1), 0)
    for kt in range(n // tt):
        tile = slice(kt * tt, (kt + 1) * tt)
        starts, rounds = _tile_windows(off_ref, pl.program_id(0), kt, n // tt, wn)

        def one_round(r, carry):
            for e in range(N_EXPERTS):
                lo, base = _window(starts[e], r, wn, cap)
                sid = base + w_col
                sid = jnp.where((sid >= lo) & (sid < lo + wn), sid, -2)
                hit = pos_ref[e:e + 1, tile] == sid
                p_scr[e * wn:(e + 1) * wn, :] = jnp.where(hit, 1.0, 0.0).astype(BF16)
                gate_out[e, pl.ds(base, wn), :] += jnp.sum(jnp.where(hit, aff_ref[e:e + 1, tile], 0.0),
                                                           axis=1, keepdims=True)
            xs = _dot(p_scr[...], h_ref[tile, :]).astype(BF16)
            for e in range(N_EXPERTS):
                _, base = _window(starts[e], r, wn, cap)
                x_out[e, pl.ds(base, wn), :] += xs[e * wn:(e + 1) * wn, :]
            return carry

        lax.fori_loop(0, rounds, one_round, 0)


def _gather(h2, pos, aff_t, off, cap):
    b, n, d = h2.shape
    wn = _window_rows(cap, n)
    grid_spec = pltpu.PrefetchScalarGridSpec(
        num_scalar_prefetch=1,
        grid=(b,),
        in_specs=[pl.BlockSpec((None, n, d), lambda i, off: (i, 0, 0)),
                  pl.BlockSpec((None, N_EXPERTS, n), lambda i, off: (i, 0, 0)),
                  pl.BlockSpec((None, N_EXPERTS, n), lambda i, off: (i, 0, 0))],
        out_specs=[pl.BlockSpec((None, N_EXPERTS, cap, d), lambda i, off: (i, 0, 0, 0)),
                   pl.BlockSpec((None, N_EXPERTS, cap, 1), lambda i, off: (i, 0, 0, 0))],
        scratch_shapes=[pltpu.VMEM((N_EXPERTS * wn, MOE_TOKEN_TILE), BF16)])
    return pl.pallas_call(
        functools.partial(_gather_kernel, cap=cap, wn=wn),
        grid_spec=grid_spec,
        out_shape=[jax.ShapeDtypeStruct((b, N_EXPERTS, cap, d), BF16),
                   jax.ShapeDtypeStruct((b, N_EXPERTS, cap, 1), F32)],
        compiler_params=_params(1),
        name="gather",
    )(off, h2, pos, aff_t)


FFN_SPLIT = 4


def _ffn_kernel(x_ref, gate_ref, wg_ref, wu_ref, wd_ref, y_out):
    bt, cap, d = x_ref.shape
    fs = wg_ref.shape[1] // FFN_SPLIT
    x = x_ref[...].reshape(bt * cap, d)
    y = jnp.zeros((bt * cap, d), F32)
    for h in range(FFN_SPLIT):
        cols = slice(h * fs, (h + 1) * fs)
        g = _dot(x, wg_ref[:, cols].astype(BF16))
        hid = (g * jax.nn.sigmoid(g)) * _dot(x, wu_ref[:, cols].astype(BF16))
        y = y + _dot(hid.astype(BF16), wd_ref[cols, :].astype(BF16))
    y = y * gate_ref[...].reshape(bt * cap, 1)
    y_out[...] = y.astype(BF16).reshape(bt, cap, d)


def _ffn(xsel, gate, wg, wu, wd):
    b, e, cap, d = xsel.shape
    bt = max(1, min(b, MOE_ROWS // cap))
    f = wg.shape[-1]
    return pl.pallas_call(
        _ffn_kernel,
        grid=(e, b // bt),
        in_specs=[pl.BlockSpec((bt, None, cap, d), lambda i, j: (j, i, 0, 0)),
                  pl.BlockSpec((bt, None, cap, 1), lambda i, j: (j, i, 0, 0)),
                  pl.BlockSpec((None, d, f), lambda i, j: (i, 0, 0)),
                  pl.BlockSpec((None, d, f), lambda i, j: (i, 0, 0)),
                  pl.BlockSpec((None, f, d), lambda i, j: (i, 0, 0))],
        out_specs=pl.BlockSpec((bt, None, cap, d), lambda i, j: (j, i, 0, 0)),
        out_shape=jax.ShapeDtypeStruct((b, e, cap, d), BF16),
        compiler_params=_params(2),
        name="ffn",
    )(xsel, gate, wg, wu, wd)


def _combine_kernel(off_ref, x1_ref, y_ref, post_ref, mod_ref, gf_ref, out_ref, ycat_scr, *, cap, wn, n_tiles):
    tn, d = x1_ref.shape
    width = N_EXPERTS * wn
    post = post_ref[...].astype(F32).astype(BF16)
    lane = lax.broadcasted_iota(jnp.int32, (N_EXPERTS, width), 1)
    erow = lax.broadcasted_iota(jnp.int32, (N_EXPERTS, width), 0)
    ids = _dot(post, jnp.where(erow == lane // wn, 1.0, 0.0).astype(BF16))
    lane1 = lax.broadcasted_iota(jnp.int32, (1, width), 1)
    starts, rounds = _tile_windows(off_ref, pl.program_id(0), pl.program_id(1), n_tiles, wn)

    def one_round(r, acc):
        sid = jnp.full((1, width), -2, jnp.int32)
        for e in range(N_EXPERTS):
            lo, base = _window(starts[e], r, wn, cap)
            ycat_scr[e * wn:(e + 1) * wn, :] = y_ref[e, pl.ds(base, wn), :]
            s = base + (lane1 - e * wn)
            sid = jnp.where((lane1 // wn == e) & (s >= lo) & (s < lo + wn), s, sid)
        hit = ids == sid.astype(F32)
        return acc + _dot(jnp.where(hit, 1.0, 0.0).astype(BF16), ycat_scr[...])

    acc = lax.fori_loop(0, rounds, one_round, jnp.zeros((tn, d), F32))
    gate2 = mod_ref[...][5:6]
    out_ref[...] = _rms_rows(x1_ref[...] + gate2 * acc, gf_ref[...])


def _combine(x1, ysel, pos_t, off, mod, gf, cap):
    b, n, d = x1.shape
    tn = MOE_TOKEN_TILE
    wn = _window_rows(cap, n)
    shared = mod.shape[0] == 1
    mod_idx = (lambda i, j, off: (0, 0, 0)) if shared else (lambda i, j, off: (i, 0, 0))
    grid_spec = pltpu.PrefetchScalarGridSpec(
        num_scalar_prefetch=1,
        grid=(b, n // tn),
        in_specs=[pl.BlockSpec((None, tn, d), lambda i, j, off: (i, j, 0)),
                  pl.BlockSpec((None, N_EXPERTS, cap, d), lambda i, j, off: (i, 0, 0, 0)),
                  pl.BlockSpec((None, tn, N_EXPERTS), lambda i, j, off: (i, j, 0)),
                  pl.BlockSpec((None, N_MOD, d), mod_idx),
                  pl.BlockSpec((1, d), lambda i, j, off: (0, 0))],
        out_specs=pl.BlockSpec((None, tn, d), lambda i, j, off: (i, j, 0)),
        scratch_shapes=[pltpu.VMEM((N_EXPERTS * wn, d), BF16)])
    return pl.pallas_call(
        functools.partial(_combine_kernel, cap=cap, wn=wn, n_tiles=n // tn),
        grid_spec=grid_spec,
        out_shape=jax.ShapeDtypeStruct((b, n, d), F32),
        compiler_params=_params(2),
        name="combine",
    )(off, x1, ysel, pos_t, mod, gf)


def _rope_tables(n_tokens):
    pairs = HEAD_DIM // 4
    rows = n_tokens // GRID_W
    row = jnp.repeat(jnp.arange(rows, dtype=F32), GRID_W)
    col = jnp.tile(jnp.arange(GRID_W, dtype=F32), rows)
    inv_freq = ROPE_THETA ** (-jnp.arange(pairs, dtype=F32) / pairs)
    ang = jnp.concatenate([row[:, None] * inv_freq, col[:, None] * inv_freq], axis=-1)
    reps = LANES // (HEAD_DIM // 2)
    return jnp.tile(jnp.cos(ang), (1, reps)), jnp.tile(jnp.sin(ang), (1, reps))


def _trunk(x, mod, w, rope_tabs, ctx_k, ctx_v, ctx_state):
    b, n, d = x.shape
    q, k, v, u = _inproj(x, mod, w["g1"], w["w_in"], w["qg"], w["kg"], w["bd"], rope_tabs)
    kv = [(k, v)]
    if ctx_k is not None:
        kv.append((ctx_k.reshape(b, -1, KV_WIDTH), ctx_v.reshape(b, -1, KV_WIDTH)))
    attn = _attention(q, kv)
    y_ssm, state = _s5_scan(u, w["s5ops"], ctx_state)
    x1, h2, aff_t = _post(x, attn, y_ssm, u, w["d_skip"], w["w_glu"], w["b_glu"], w["w_out"], mod,
                          w["g2"], w["w_router_t"])
    cap = CAPACITY_FACTOR * n // N_EXPERTS
    pos, off = _route(aff_t, cap)
    xsel, gate = _gather(h2, pos, aff_t, off, cap)
    ysel = _ffn(xsel, gate, w["w_gate"], w["w_up"], w["w_down"])
    out = _combine(x1, ysel, pos.transpose(0, 2, 1), off, mod, w["gf"], cap)
    return out, k, v, state


def kernel(x_prompt, x_sample, cache_k, cache_v, state_ssm, c, c_ctx, norm1_g, norm2_g, w_mod, b_mod, w_in, q_norm_g, k_norm_g, ssm_a_re, ssm_a_im, ssm_log_dt, ssm_b_re, ssm_b_im, ssm_c_re, ssm_c_im, ssm_d, w_glu, b_glu, w_out, w_router, w_gate, w_up, w_down, final_norm_g):
    assert norm1_g.shape[0] == 1, "single trunk layer"
    d = x_prompt.shape[-1]
    n_dec = c.shape[0]
    cond = jnp.concatenate([c_ctx[None, :], c, jnp.zeros((16 - 1 - n_dec, d), F32)], axis=0)
    mod = _modulation(cond, w_mod[0], b_mod[0]).reshape(16, N_MOD, d)
    head = jnp.arange(REP * HEAD_DIM) // HEAD_DIM
    w = {
        "g1": norm1_g, "g2": norm2_g, "gf": final_norm_g.reshape(1, d),
        "w_in": w_in[0].astype(BF16),
        "qg": jnp.tile(q_norm_g, (1, REP)),
        "kg": jnp.tile(k_norm_g, (1, REP)),
        "bd": (head[:, None] == head[None, :]).astype(BF16),
        "s5ops": _s5_prep(ssm_a_re[0], ssm_a_im[0], ssm_log_dt[0], ssm_b_re[0], ssm_b_im[0],
                          ssm_c_re[0], ssm_c_im[0]),
        "d_skip": ssm_d, "w_glu": w_glu[0].astype(BF16), "b_glu": b_glu,
        "w_out": w_out[0].astype(BF16),
        "w_router_t": w_router[0].T.astype(BF16),
        "w_gate": w_gate[0], "w_up": w_up[0], "w_down": w_down[0],
    }
    y_p, k_ctx, v_ctx, st_ctx = _trunk(x_prompt, mod[0:1], w, None, None, None, None)
    y_s, _, _, _ = _trunk(x_sample, mod[1:1 + n_dec], w, _rope_tables(x_sample.shape[1]),
                          cache_k[:, 0], cache_v[:, 0], state_ssm[:, 0])
    bp, np_ = x_prompt.shape[:2]
    new_k = k_ctx.reshape(bp, 1, np_, N_KV_HEADS, HEAD_DIM)
    new_v = v_ctx.reshape(bp, 1, np_, N_KV_HEADS, HEAD_DIM)
    return (y_p, y_s, new_k, new_v, st_ctx[:, None])
```

```python
import functools
import math

import jax
import jax.numpy as jnp
from jax import lax
from jax.experimental import pallas as pl
from jax.experimental.pallas import tpu as pltpu

F32 = jnp.float32
BF16 = jnp.bfloat16

HEAD_DIM = 64
N_HEADS = 8
N_KV_HEADS = 2
REP = N_HEADS // N_KV_HEADS
ATTN_WIDTH = N_HEADS * HEAD_DIM
KV_WIDTH = N_KV_HEADS * HEAD_DIM
SSM_GROUP = 16
SSM_STATE = 64
N_EXPERTS = 16
CAPACITY_FACTOR = 2
GRID_W = 64
ROPE_THETA = 10000.0
NORM_EPS = 1e-6
N_MOD = 6
CHUNK = 16
LANES = 128
VMEM_LIMIT = 56 * 1024 * 1024
MOE_ROWS = 1024
TOKEN_ROWS = 1024


def _params(n_grid_axes, vmem=VMEM_LIMIT):
    return pltpu.CompilerParams(dimension_semantics=("arbitrary",) * n_grid_axes,
                                vmem_limit_bytes=vmem)


def _chunked_spec(width, tm):
    return pl.BlockSpec((width // LANES, tm // CHUNK, None, CHUNK, LANES), lambda i, j: (0, j, i, 0, 0))


def _dot(a, b):
    return jnp.dot(a, b, preferred_element_type=F32)


def _dot_nt(a, b):
    return lax.dot_general(a, b, (((1,), (1,)), ((), ())), preferred_element_type=F32)


def _rms_rows(x, g):
    return x * lax.rsqrt(jnp.mean(x * x, axis=-1, keepdims=True) + NORM_EPS) * g


def _mod_kernel(cond_ref, w_ref, b_ref, out_ref):
    c = cond_ref[...]
    s = (c * jax.nn.sigmoid(c)).astype(BF16)
    out_ref[...] = _dot(s, w_ref[...].astype(BF16)) + b_ref[...]


def _modulation(cond, w_mod, b_mod):
    rows, d = cond.shape
    n = w_mod.shape[1]
    tn = 1024
    return pl.pallas_call(
        _mod_kernel,
        grid=(n // tn,),
        in_specs=[pl.BlockSpec((rows, d), lambda j: (0, 0)),
                  pl.BlockSpec((d, tn), lambda j: (0, j)),
                  pl.BlockSpec((1, tn), lambda j: (0, j))],
        out_specs=pl.BlockSpec((rows, tn), lambda j: (0, j)),
        out_shape=jax.ShapeDtypeStruct((rows, n), F32),
        compiler_params=_params(1),
        name="mod",
    )(cond, w_mod, b_mod.reshape(1, n))


def _inproj_kernel(*refs, rope):
    if rope:
        (x_ref, mod_ref, g1_ref, win_ref, qg_ref, kg_ref, bd_ref, cos_ref, sin_ref,
         q_out, k_out, v_out, u_out) = refs
    else:
        (x_ref, mod_ref, g1_ref, win_ref, qg_ref, kg_ref, bd_ref,
         q_out, k_out, v_out, u_out) = refs
    x = x_ref[...]
    m = mod_ref[...]
    shift1, scale1 = m[0:1], m[1:2]
    h = _rms_rows(x, g1_ref[...]) * (1.0 + scale1) + shift1
    z = _dot(h.astype(BF16), win_ref[...])
    bd = bd_ref[...]
    if rope:
        cos = cos_ref[...]
        sin = sin_ref[...]
        lane = lax.broadcasted_iota(jnp.int32, cos.shape, 1)
        first_half = (lane % HEAD_DIM) < (HEAD_DIM // 2)

    def head_norm(zc, g):
        w = zc.shape[1]
        ss = _dot((zc * zc).astype(BF16), bd[:w, :w])
        return zc * lax.rsqrt(ss * (1.0 / HEAD_DIM) + NORM_EPS) * g[:, :w]

    def rope_lanes(zn):
        if not rope:
            return zn
        rot = jnp.where(first_half,
                        -pltpu.roll(zn, LANES - HEAD_DIM // 2, 1),
                        pltpu.roll(zn, HEAD_DIM // 2, 1))
        return zn * cos + rot * sin

    q_scale = HEAD_DIM ** -0.5 * math.log2(math.e)
    nw = bd.shape[0]
    for i in range(0, ATTN_WIDTH, nw):
        qn = head_norm(z[:, i:i + nw], qg_ref[...])
        for j in range(0, nw, LANES):
            q_out[:, i + j:i + j + LANES] = (rope_lanes(qn[:, j:j + LANES]) * q_scale).astype(BF16)
    k_out[...] = rope_lanes(head_norm(z[:, ATTN_WIDTH:ATTN_WIDTH + KV_WIDTH], kg_ref[...]))
    v_out[...] = z[:, ATTN_WIDTH + KV_WIDTH:ATTN_WIDTH + 2 * KV_WIDTH]
    u0 = ATTN_WIDTH + 2 * KV_WIDTH
    for gb in range(u_out.shape[0]):
        u_out[gb] = z[:, u0 + gb * LANES:u0 + (gb + 1) * LANES].reshape(u_out.shape[1:])


def _inproj(x, mod, g1, w_in_bf, qg, kg, bd, rope_tabs):
    b, n, d = x.shape
    tm = min(n, TOKEN_ROWS)
    ssm_w = w_in_bf.shape[1] - ATTN_WIDTH - 2 * KV_WIDTH
    shared = mod.shape[0] == 1
    mod_idx = (lambda i, j: (0, 0, 0)) if shared else (lambda i, j: (i, 0, 0))
    const2 = lambda i, j: (0, 0)
    in_specs = [pl.BlockSpec((None, tm, d), lambda i, j: (i, j, 0)),
                pl.BlockSpec((None, N_MOD, d), mod_idx),
                pl.BlockSpec((1, d), const2),
                pl.BlockSpec(w_in_bf.shape, const2),
                pl.BlockSpec(qg.shape, const2),
                pl.BlockSpec(kg.shape, const2),
                pl.BlockSpec(bd.shape, const2)]
    args = [x, mod, g1, w_in_bf, qg, kg, bd]
    if rope_tabs is not None:
        in_specs += [pl.BlockSpec((tm, LANES), lambda i, j: (j, 0))] * 2
        args += list(rope_tabs)
    tok = lambda w: pl.BlockSpec((None, tm, w), lambda i, j: (i, j, 0))
    return pl.pallas_call(
        functools.partial(_inproj_kernel, rope=rope_tabs is not None),
        grid=(b, n // tm),
        in_specs=in_specs,
        out_specs=[tok(ATTN_WIDTH), tok(KV_WIDTH), tok(KV_WIDTH), _chunked_spec(ssm_w, tm)],
        out_shape=[jax.ShapeDtypeStruct((b, n, ATTN_WIDTH), BF16),
                   jax.ShapeDtypeStruct((b, n, KV_WIDTH), F32),
                   jax.ShapeDtypeStruct((b, n, KV_WIDTH), F32),
                   jax.ShapeDtypeStruct((ssm_w // LANES, n // CHUNK, b, CHUNK, LANES), F32)],
        compiler_params=_params(2),
        name="inproj",
    )(*args)


KV_FILL_ROWS = 256
ATTN_Q_ROWS = 512
ATTN_KEY_SLAB = 256


def _attn_kernel(*refs, n_src):
    q_ref = refs[0]
    kv_refs = refs[1:1 + 2 * n_src]
    out_ref, k4_scr, vt_scr = refs[1 + 2 * n_src:]
    gw = REP * HEAD_DIM

    @pl.when(pl.program_id(1) == 0)
    def _():
        lane = lax.broadcasted_iota(jnp.int32, (KV_FILL_ROWS, KV_WIDTH), 1)
        first = lane < HEAD_DIM
        row0 = 0
        for k_src, v_src in zip(kv_refs[0::2], kv_refs[1::2]):
            for r in range(0, k_src.shape[0], KV_FILL_ROWS):
                rows = slice(row0 + r, row0 + r + KV_FILL_ROWS)
                x = k_src[r:r + KV_FILL_ROWS, :]
                sw = pltpu.roll(x, HEAD_DIM, 1)
                for g, xx in enumerate((jnp.where(first, x, sw), jnp.where(first, sw, x))):
                    xx = xx.astype(BF16)
                    k4_scr[g, rows, :] = jnp.concatenate([xx, xx], axis=1)
                vt = v_src[r:r + KV_FILL_ROWS, :].T
                for g in range(N_KV_HEADS):
                    vt_scr[g, :, rows] = vt[g * HEAD_DIM:(g + 1) * HEAD_DIM, :].astype(BF16)
            row0 += k_src.shape[0]

    lane_head = lax.broadcasted_iota(jnp.int32, (q_ref.shape[0], gw), 1) // HEAD_DIM

    n_keys = k4_scr.shape[1]
    slab = min(n_keys, ATTN_KEY_SLAB)

    def scores(h):
        g, r = divmod(h, REP)
        qg = q_ref[:, g * gw:(g + 1) * gw]
        qm = jnp.where(lane_head == r, qg, jnp.zeros_like(qg))
        sts = [_dot_nt(k4_scr[g, i:i + slab, :], qm) for i in range(0, n_keys, slab)]
        m = functools.reduce(jnp.maximum, [jnp.max(s, axis=0, keepdims=True) for s in sts])
        return sts, m

    heads = []
    nxt = scores(0)
    for h in range(N_HEADS):
        sts, m = nxt
        if h + 1 < N_HEADS:
            nxt = scores(h + 1)
        l = jnp.zeros_like(m)
        o = jnp.zeros((HEAD_DIM, q_ref.shape[0]), F32)
        for i, st in enumerate(sts):
            pt = jnp.exp2(st - m)
            l = l + jnp.sum(pt, axis=0, keepdims=True)
            o = o + _dot(vt_scr[h // REP, :, i * slab:(i + 1) * slab], pt.astype(BF16))
        heads.append(o / l)
    for i in range(0, N_HEADS, 2):
        pair = jnp.concatenate(heads[i:i + 2], axis=0)
        out_ref[:, i * HEAD_DIM:(i + 2) * HEAD_DIM] = pair.T.astype(BF16)


def _attention(q, kv_sources):
    b, n, _ = q.shape
    s = sum(k.shape[1] for k, _ in kv_sources)
    tq = min(n, ATTN_Q_ROWS)
    gw = REP * HEAD_DIM
    in_specs = [pl.BlockSpec((None, tq, ATTN_WIDTH), lambda i, j: (i, j, 0))]
    args = [q]
    for k, v in kv_sources:
        in_specs += [pl.BlockSpec((None, k.shape[1], KV_WIDTH), lambda i, j: (i, 0, 0))] * 2
        args += [k, v]
    return pl.pallas_call(
        functools.partial(_attn_kernel, n_src=len(kv_sources)),
        grid=(b, n // tq),
        in_specs=in_specs,
        out_specs=pl.BlockSpec((None, tq, ATTN_WIDTH), lambda i, j: (i, j, 0)),
        out_shape=jax.ShapeDtypeStruct((b, n, ATTN_WIDTH), BF16),
        scratch_shapes=[pltpu.VMEM((N_KV_HEADS, s, gw), BF16), pltpu.VMEM((N_KV_HEADS, HEAD_DIM, s), BF16)],
        compiler_params=_params(2),
        name="attn",
    )(*args)


def _cmul(ar, ai, br, bi):
    return ar * br - ai * bi, ar * bi + ai * br


def _cpow(ar, ai, e, shape):
    rr = jnp.ones(shape, F32)
    ri = jnp.zeros(shape, F32)
    pr = jnp.broadcast_to(ar, shape)
    pi = jnp.broadcast_to(ai, shape)
    e = jnp.broadcast_to(e, shape)
    for k in range(CHUNK.bit_length()):
        nr, ni = _cmul(rr, ri, pr, pi)
        bit = ((e >> k) & 1) == 1
        rr = jnp.where(bit, nr, rr)
        ri = jnp.where(bit, ni, ri)
        pr, pi = _cmul(pr, pi, pr, pi)
    return rr, ri


def _zoh(ar, ai, dt):
    mag = jnp.exp(ar * dt)
    abr = mag * jnp.cos(ai * dt)
    abi = mag * jnp.sin(ai * dt)
    den = ar * ar + ai * ai
    nre = abr - 1.0
    nim = abi
    cre = (nre * ar + nim * ai) / den
    cim = (nim * ar - nre * ai) / den
    return abr, abi, cre, cim


def _s5prep_kernel(arr_ref, air_ref, ldt_ref, btr_ref, bti_ref, ctr_ref, cti_ref,
                   arp_ref, aip_ref, ldtp_ref, btpr_ref, btpi_ref,
                   m_out, p_out, q_out, a16r_out, a16i_out):
    P, I, T = SSM_STATE, SSM_GROUP, CHUNK
    W = T * I
    hi = lax.Precision.HIGHEST
    blk = lax.broadcasted_iota(jnp.int32, (1, W), 1) // I
    lane = lax.broadcasted_iota(jnp.int32, (I, W), 1)
    zero_q = jnp.zeros((P, W), F32)
    diag = lax.broadcasted_iota(jnp.int32, (P, P), 0) == lax.broadcasted_iota(jnp.int32, (P, P), 1)
    for gg in range(2):
        kalls = []
        for d in range(2):
            dt = jnp.exp(ldt_ref[d, gg])
            abr_r, abi_r, cre_r, cim_r = _zoh(arr_ref[d, gg], air_ref[d, gg], dt)
            bbr, bbi = _cmul(cre_r, cim_r, btr_ref[d, gg], bti_ref[d, gg])
            abr_c = jnp.sum(jnp.where(diag, abr_r, 0.0), axis=1, keepdims=True)
            abi_c = jnp.sum(jnp.where(diag, abi_r, 0.0), axis=1, keepdims=True)

            e_w = blk if d == 0 else (T - 1) - blk
            pwr, pwi = _cpow(abr_c, abi_c, e_w, (P, W))
            w_re, w_im = _cmul(pwr, pwi, ctr_ref[d, gg], cti_ref[d, gg])
            kalls.append(jnp.dot(bbr, w_re, precision=hi, preferred_element_type=F32)
                         - jnp.dot(bbi, w_im, precision=hi, preferred_element_type=F32))

            q_re, q_im = _cmul(w_re, w_im, jnp.broadcast_to(abr_c, (P, W)), jnp.broadcast_to(abi_c, (P, W)))
            pad = lambda q: jnp.concatenate([q, zero_q] if gg == 0 else [zero_q, q], axis=0)
            q_out[gg, (2 * d) * 2 * P:(2 * d + 1) * 2 * P, :] = pad(q_re).astype(BF16)
            q_out[gg, (2 * d + 1) * 2 * P:(2 * d + 2) * 2 * P, :] = pad(-q_im).astype(BF16)

        for s in range(T):
            fwd = jnp.where(lane >= I * s, pltpu.roll(kalls[0], I * s, 1) if s else kalls[0], 0.0)
            sh = (I * (s + 1)) % W
            bwd = jnp.where(lane < I * (s + 1), pltpu.roll(kalls[1], sh, 1) if sh else kalls[1], 0.0)
            m_out[gg, s * I:(s + 1) * I, :] = (fwd + bwd).astype(BF16)

    srow = lax.broadcasted_iota(jnp.int32, (W, 1), 0) // I
    lane_p = lax.broadcasted_iota(jnp.int32, (W, 2 * P), 1)
    for d in range(2):
        dt = jnp.exp(ldtp_ref[d])
        abr, abi, cre, cim = _zoh(arp_ref[d], aip_ref[d], dt)
        bbr, bbi = _cmul(cre, cim, btpr_ref[d], btpi_ref[d])
        e_p = (T - 1) - srow if d == 0 else srow
        ppr, ppi = _cpow(abr, abi, e_p, (W, 2 * P))
        p_re, p_im = _cmul(ppr, ppi, jnp.concatenate([bbr] * T, axis=0), jnp.concatenate([bbi] * T, axis=0))
        for gg in range(2):
            own = (lane_p < P) if gg == 0 else (lane_p >= P)
            p_out[gg, :, (2 * d) * 2 * P:(2 * d + 1) * 2 * P] = jnp.where(own, p_re, 0.0).astype(BF16)
            p_out[gg, :, (2 * d + 1) * 2 * P:(2 * d + 2) * 2 * P] = jnp.where(own, p_im, 0.0).astype(BF16)
        a16r, a16i = _cpow(abr, abi, jnp.full((1, 2 * P), T, jnp.int32), (1, 2 * P))
        a16r_out[d] = a16r
        a16i_out[d] = a16i


def _s5_prep(a_re, a_im, log_dt, b_re, b_im, c_re, c_im):
    _, G, P = a_re.shape
    I, T = SSM_GROUP, CHUNK
    W = T * I
    row = lambda a: a.reshape(2, G, 1, P)
    bt = lambda b: jnp.swapaxes(b, 2, 3)
    ct = lambda c: jnp.tile(jnp.swapaxes(c, 2, 3), (1, 1, 1, T))
    prow = lambda a: a.reshape(2, G // 2, 1, 2 * P)
    pbt = lambda b: (bt(b).reshape(2, G // 2, 2, I, P).transpose(0, 1, 3, 2, 4).reshape(2, G // 2, I, 2 * P))
    spec = lambda r, c: pl.BlockSpec((2, 2, r, c), lambda j: (0, j, 0, 0))
    pspec = lambda r, c: pl.BlockSpec((2, None, r, c), lambda j: (0, j, 0, 0))
    return pl.pallas_call(
        _s5prep_kernel,
        grid=(G // 2,),
        in_specs=[spec(1, P), spec(1, P), spec(1, 1),
                  spec(I, P), spec(I, P), spec(P, W), spec(P, W),
                  pspec(1, 2 * P), pspec(1, 2 * P), pspec(1, 2 * P), pspec(I, 2 * P), pspec(I, 2 * P)],
        out_specs=[pl.BlockSpec((2, W, W), lambda j: (j, 0, 0)),
                   pl.BlockSpec((2, W, 8 * P), lambda j: (j, 0, 0)),
                   pl.BlockSpec((2, 8 * P, W), lambda j: (j, 0, 0)),
                   pspec(1, 2 * P), pspec(1, 2 * P)],
        out_shape=[jax.ShapeDtypeStruct((G, W, W), BF16),
                   jax.ShapeDtypeStruct((G, W, 8 * P), BF16),
                   jax.ShapeDtypeStruct((G, 8 * P, W), BF16),
                   jax.ShapeDtypeStruct((2, G // 2, 1, 2 * P), F32),
                   jax.ShapeDtypeStruct((2, G // 2, 1, 2 * P), F32)],
        compiler_params=_params(1),
        name="s5prep",
    )(row(a_re), row(a_im), log_dt.reshape(2, G, 1, 1),
      bt(b_re), bt(b_im), ct(c_re), ct(c_im),
      prow(a_re), prow(a_im), prow(jnp.repeat(log_dt, P, axis=-1)), pbt(b_re), pbt(b_im))


RELAY_ROWS = 32
S5_OUT_ROWS = 256


def _block_transpose(xs, masks):
    n = len(xs)
    rolled = []
    for delta in range(n):
        comp = xs[delta]
        for q in range(1, n):
            comp = jnp.where(masks[q], xs[(q + delta) % n], comp)
        rolled.append(pltpu.roll(comp, SSM_GROUP * delta, 1) if delta else comp)
    out = []
    for a in range(n):
        y = rolled[(-a) % n]
        for p in range(1, n):
            y = jnp.where(masks[p], rolled[(p - a) % n], y)
        out.append(y)
    return out


def _s5_kernel(*refs, batch, has_h0):
    if has_h0:
        (u_ref, m_ref, p_ref, q_ref, a16r_ref, a16i_ref, h0_ref,
         y_ref, fin_ref, z_scr, s_scr, yt_scr) = refs
    else:
        (u_ref, m_ref, p_ref, q_ref, a16r_ref, a16i_ref,
         y_ref, fin_ref, z_scr, s_scr, yt_scr) = refs
    I, T, RT, RC = SSM_GROUP, CHUNK, RELAY_ROWS, S5_OUT_ROWS
    gpb = LANES // I
    halves = T * I // LANES
    rows = u_ref.shape[0] // T
    n_chunks = rows // batch
    lane_blk = lax.broadcasted_iota(jnp.int32, (RT, LANES), 1) // I
    masks = [lane_blk == p for p in range(gpb)]

    def to_groups(i, carry):
        r0 = pl.multiple_of(i * RT, RT)
        for h in range(halves):
            xs = [u_ref[pl.ds(r0 * T + p + gpb * h, RT, stride=T), :] for p in range(gpb)]
            for g, zg in enumerate(_block_transpose(xs, masks)):
                z_scr[g, pl.ds(r0, RT), h * LANES:(h + 1) * LANES] = zg.astype(BF16)
        return carry

    lax.fori_loop(0, rows // RT, to_groups, 0)

    npair = gpb // 2
    for j in range(npair):
        s_scr[j] = _dot(z_scr[2 * j], p_ref[2 * j]) + _dot(z_scr[2 * j + 1], p_ref[2 * j + 1])
    part = lambda d, c: slice((2 * d + c) * LANES, (2 * d + c + 1) * LANES)

    for j0 in range(npair):
        chains = [(j0, d) for d in range(2)]
        a = [(jnp.broadcast_to(a16r_ref[d, j], (batch, LANES)),
              jnp.broadcast_to(a16i_ref[d, j], (batch, LANES))) for j, d in chains]
        if has_h0:
            init = tuple(h0_ref[j, d, c] for j, d in chains for c in range(2))
        else:
            init = tuple(jnp.zeros((batch, LANES), F32) for _ in range(2 * len(chains)))

        def step(k, carry):
            out = []
            for i, (j, d) in enumerate(chains):
                hr, hi = carry[2 * i], carry[2 * i + 1]
                c = k if d == 0 else n_chunks - 1 - k
                rs = pl.ds(pl.multiple_of(c * batch, batch), batch)
                sr = s_scr[j, rs, part(d, 0)]
                si = s_scr[j, rs, part(d, 1)]
                s_scr[j, rs, part(d, 0)] = hr
                s_scr[j, rs, part(d, 1)] = hi
                ar, ai = a[i]
                out.append(ar * hr - ai * hi + sr)
                out.append(ar * hi + ai * hr + si)
            return tuple(out)

        fin = lax.fori_loop(0, n_chunks, step, init)
        for i, (j, d) in enumerate(chains):
            fin_ref[j, d, 0] = fin[2 * i]
            fin_ref[j, d, 1] = fin[2 * i + 1]

    def out_tile(i, carry):
        r0 = pl.multiple_of(i * RC, RC)
        rs = pl.ds(r0, RC)
        for g in range(gpb):
            yt_scr[g] = _dot(z_scr[g, rs, :], m_ref[g]) + _dot(s_scr[g // 2, rs, :].astype(BF16), q_ref[g])

        def to_tokens(k, carry2):
            q0 = pl.multiple_of(k * RT, RT)
            for h in range(halves):
                xs = [yt_scr[g, pl.ds(q0, RT), h * LANES:(h + 1) * LANES] for g in range(gpb)]
                for p, yp in enumerate(_block_transpose(xs, masks)):
                    y_ref[pl.ds((r0 + q0) * T + p + gpb * h, RT, stride=T), :] = yp
            return carry2

        lax.fori_loop(0, RC // RT, to_tokens, 0)
        return carry

    lax.fori_loop(0, rows // RC, out_tile, 0)


def _s5_scan(u_blk, ops, h0):
    nb, nc, b, _, _ = u_blk.shape
    gpb = LANES // SSM_GROUP
    npair = gpb // 2
    W = CHUNK * SSM_GROUP
    P = SSM_STATE
    rows = nc * b
    m, p_op, q_op, a16r, a16i = ops
    pspec = pl.BlockSpec((2, npair, 1, 2 * P), lambda j: (0, j, 0, 0))
    act = lambda **kw: pl.BlockSpec((None, rows * CHUNK, LANES), lambda j: (j, 0, 0), **kw)
    st = pl.BlockSpec((None, npair, 2, 2, b, 2 * P), lambda j: (j, 0, 0, 0, 0, 0))
    in_specs = [act(pipeline_mode=pl.Buffered(1)), pl.BlockSpec((gpb, W, W), lambda j: (j, 0, 0)),
                pl.BlockSpec((gpb, W, 8 * P), lambda j: (j, 0, 0)),
                pl.BlockSpec((gpb, 8 * P, W), lambda j: (j, 0, 0)), pspec, pspec]
    args = [u_blk.reshape(nb, rows * CHUNK, LANES), m, p_op, q_op, a16r, a16i]
    if h0 is not None:
        in_specs.append(st)
        args.append(h0.reshape(b, 2, 2, nb, npair, 2, P).transpose(3, 4, 1, 2, 0, 5, 6)
                    .reshape(nb, npair, 2, 2, b, 2 * P))
    y, fin = pl.pallas_call(
        functools.partial(_s5_kernel, batch=b, has_h0=h0 is not None),
        grid=(nb,),
        in_specs=in_specs,
        out_specs=[act(), st],
        out_shape=[jax.ShapeDtypeStruct((nb, rows * CHUNK, LANES), F32),
                   jax.ShapeDtypeStruct((nb, npair, 2, 2, b, 2 * P), F32)],
        scratch_shapes=[pltpu.VMEM((gpb, rows, W), BF16),
                        pltpu.VMEM((npair, rows, 8 * P), F32),
                        pltpu.VMEM((gpb, S5_OUT_ROWS, W), F32)],
        compiler_params=_params(1),
        name="s5",
    )(*args)
    fin = (fin.reshape(nb, npair, 2, 2, b, 2, P).transpose(4, 2, 3, 0, 1, 5, 6)
           .reshape(b, 2, 2, nb * gpb, P))
    return y.reshape(u_blk.shape), fin


def _post_kernel(x_ref, attn_ref, y_ref, u_ref, d_ref, wglu_ref, bglu_ref, wout_ref, mod_ref, g2_ref,
                 wrt_ref, x1_out, h2_out, aff_out):
    m = mod_ref[...]
    gate1, shift2, scale2 = m[2:3], m[3:4], m[4:5]
    tm = x_ref.shape[0]
    tokens = lambda r: jnp.concatenate([r[gb].reshape(tm, LANES) for gb in range(r.shape[0])], axis=1)
    y = tokens(y_ref) + tokens(u_ref) * d_ref[...]
    y = jax.nn.gelu(y)
    y = y * jax.nn.sigmoid(_dot(y.astype(BF16), wglu_ref[...]) + bglu_ref[...])
    aw = attn_ref.shape[1]
    proj = _dot(attn_ref[...], wout_ref[:aw, :]) + _dot(y.astype(BF16), wout_ref[aw:, :])
    x1 = x_ref[...] + gate1 * proj
    x1_out[...] = x1
    h2 = (_rms_rows(x1, g2_ref[...]) * (1.0 + scale2) + shift2).astype(BF16)
    h2_out[...] = h2
    logits = _dot_nt(wrt_ref[...], h2)
    e = jnp.exp(logits - jnp.max(logits, axis=0, keepdims=True))
    aff_out[...] = e / jnp.sum(e, axis=0, keepdims=True)


def _post(x, attn, y_ssm, u, d_skip, wglu_bf, b_glu, wout_bf, mod, g2, wrt_bf):
    b, n, d = x.shape
    tm = min(n, TOKEN_ROWS)
    sw = u.shape[0] * LANES
    shared = mod.shape[0] == 1
    mod_idx = (lambda i, j: (0, 0, 0)) if shared else (lambda i, j: (i, 0, 0))
    const2 = lambda i, j: (0, 0)
    tok = lambda w: pl.BlockSpec((None, tm, w), lambda i, j: (i, j, 0))
    return pl.pallas_call(
        _post_kernel,
        grid=(b, n // tm),
        in_specs=[tok(d), tok(attn.shape[-1]), _chunked_spec(sw, tm), _chunked_spec(sw, tm),
                  pl.BlockSpec((1, sw), const2),
                  pl.BlockSpec(wglu_bf.shape, const2),
                  pl.BlockSpec((1, sw), const2),
                  pl.BlockSpec(wout_bf.shape, const2),
                  pl.BlockSpec((None, N_MOD, d), mod_idx),
                  pl.BlockSpec((1, d), const2),
                  pl.BlockSpec(wrt_bf.shape, const2)],
        out_specs=[tok(d), tok(d), pl.BlockSpec((None, N_EXPERTS, tm), lambda i, j: (i, 0, j))],
        out_shape=[jax.ShapeDtypeStruct((b, n, d), F32),
                   jax.ShapeDtypeStruct((b, n, d), BF16),
                   jax.ShapeDtypeStruct((b, N_EXPERTS, n), F32)],
        compiler_params=_params(2),
        name="post",
    )(x, attn, y_ssm, u, d_skip, wglu_bf, b_glu, wout_bf, mod, g2, wrt_bf)


MOE_TOKEN_TILE = 256


def _route_kernel(aff_ref, pos_out, off_out, *, cap):
    a = aff_ref[...]
    rows, n = a.shape
    capf = jnp.float32(cap)

    def count(mask):
        return jnp.sum(jnp.where(mask, 1.0, 0.0), axis=1, keepdims=True)

    def search(i, bits):
        cand = bits | (jnp.int32(1) << (30 - i))
        ok = count(a >= lax.bitcast_convert_type(cand, F32)) >= capf
        return jnp.where(ok, cand, bits)

    thr_bits = lax.fori_loop(0, 31, search, jnp.zeros((rows, 1), jnp.int32))
    thr = lax.bitcast_convert_type(thr_bits, F32)
    gt = a > thr
    eq = a == thr
    before = (lax.broadcasted_iota(jnp.int32, (n, n), 0)
              < lax.broadcasted_iota(jnp.int32, (n, n), 1))
    tri = jnp.where(before, 1.0, 0.0).astype(BF16)
    eq_rank = _dot(jnp.where(eq, 1.0, 0.0).astype(BF16), tri)
    sel = gt | (eq & (eq_rank < capf - count(gt)))
    sel_bf = jnp.where(sel, 1.0, 0.0).astype(BF16)
    pos = _dot(sel_bf, tri)
    pos_out[...] = jnp.where(sel, pos, -1.0).astype(jnp.int32)
    tile_start = lax.broadcasted_iota(jnp.int32, (n, LANES), 1) * MOE_TOKEN_TILE
    in_front = lax.broadcasted_iota(jnp.int32, (n, LANES), 0) < tile_start
    off_out[...] = _dot(sel_bf, jnp.where(in_front, 1.0, 0.0).astype(BF16)).astype(jnp.int32)


def _route(aff_t, cap):
    b, e, n = aff_t.shape
    rows = b * e
    nt = n // MOE_TOKEN_TILE
    pos, off = pl.pallas_call(
        functools.partial(_route_kernel, cap=cap),
        grid=(1,),
        in_specs=[pl.BlockSpec((rows, n), lambda i: (0, 0))],
        out_specs=[pl.BlockSpec((rows, n), lambda i: (0, 0)), pl.BlockSpec((rows, LANES), lambda i: (0, 0))],
        out_shape=[jax.ShapeDtypeStruct((rows, n), jnp.int32), jax.ShapeDtypeStruct((rows, LANES), jnp.int32)],
        compiler_params=_params(1),
        name="route",
    )(aff_t.reshape(rows, n))
    return pos.reshape(b, e, n), off[:, :nt + 1].reshape(-1)


SLOT_ALIGN = 16


def _window_rows(cap, n):
    mean = cap * MOE_TOKEN_TILE // n
    return min(cap, max(SLOT_ALIGN, -(-2 * mean // SLOT_ALIGN) * SLOT_ALIGN))


def _tile_windows(off_ref, request, tile, n_tiles, wn):
    starts, rounds = [], jnp.int32(0)
    for e in range(N_EXPERTS):
        i = (request * N_EXPERTS + e) * (n_tiles + 1) + tile
        start = (off_ref[i] // SLOT_ALIGN) * SLOT_ALIGN
        starts.append(start)
        rounds = jnp.maximum(rounds, (off_ref[i + 1] - start + wn - 1) // wn)
    return starts, rounds


def _window(start, r, wn, cap):
    lo = start + r * wn
    base = pl.multiple_of(jnp.minimum(lo, cap - wn), SLOT_ALIGN)
    return lo, base


def _gather_kernel(off_ref, h_ref, pos_ref, aff_ref, x_out, gate_out, p_scr, *, cap, wn):
    n = h_ref.shape[0]
    tt = MOE_TOKEN_TILE
    x_out[...] = jnp.zeros_like(x_out)
    gate_out[...] = jnp.zeros_like(gate_out)
    w_col = lax.broadcasted_iota(jnp.int32, (wn, 1), 0)
    for kt in range(n // tt):
        tile = slice(kt * tt, (kt + 1) * tt)
        starts, rounds = _tile_windows(off_ref, pl.program_id(0), kt, n // tt, wn)

        def one_round(r, carry):
            for e in range(N_EXPERTS):
                lo, base = _window(starts[e], r, wn, cap)
                sid = base + w_col
                sid = jnp.where((sid >= lo) & (sid < lo + wn), sid, -2)
                hit = pos_ref[e:e + 1, tile] == sid
                p_scr[e * wn:(e + 1) * wn, :] = jnp.where(hit, 1.0, 0.0).astype(BF16)
                gate_out[e, pl.ds(base, wn), :] += jnp.sum(jnp.where(hit, aff_ref[e:e + 1, tile], 0.0),
                                                           axis=1, keepdims=True)
            xs = _dot(p_scr[...], h_ref[tile, :]).astype(BF16)
            for e in range(N_EXPERTS):
                _, base = _window(starts[e], r, wn, cap)
                x_out[e, pl.ds(base, wn), :] += xs[e * wn:(e + 1) * wn, :]
            return carry

        lax.fori_loop(0, rounds, one_round, 0)


def _gather(h2, pos, aff_t, off, cap):
    b, n, d = h2.shape
    wn = _window_rows(cap, n)
    grid_spec = pltpu.PrefetchScalarGridSpec(
        num_scalar_prefetch=1,
        grid=(b,),
        in_specs=[pl.BlockSpec((None, n, d), lambda i, off: (i, 0, 0)),
                  pl.BlockSpec((None, N_EXPERTS, n), lambda i, off: (i, 0, 0)),
                  pl.BlockSpec((None, N_EXPERTS, n), lambda i, off: (i, 0, 0))],
        out_specs=[pl.BlockSpec((None, N_EXPERTS, cap, d), lambda i, off: (i, 0, 0, 0)),
                   pl.BlockSpec((None, N_EXPERTS, cap, 1), lambda i, off: (i, 0, 0, 0))],
        scratch_shapes=[pltpu.VMEM((N_EXPERTS * wn, MOE_TOKEN_TILE), BF16)])
    return pl.pallas_call(
        functools.partial(_gather_kernel, cap=cap, wn=wn),
        grid_spec=grid_spec,
        out_shape=[jax.ShapeDtypeStruct((b, N_EXPERTS, cap, d), BF16),
                   jax.ShapeDtypeStruct((b, N_EXPERTS, cap, 1), F32)],
        compiler_params=_params(1),
        name="gather",
    )(off, h2, pos, aff_t)


FFN_SPLIT = 4


def _ffn_kernel(x_ref, gate_ref, wg_ref, wu_ref, wd_ref, y_out):
    bt, cap, d = x_ref.shape
    fs = wg_ref.shape[1] // FFN_SPLIT
    x = x_ref[...].reshape(bt * cap, d)
    y = jnp.zeros((bt * cap, d), F32)
    for h in range(FFN_SPLIT):
        cols = slice(h * fs, (h + 1) * fs)
        g = _dot(x, wg_ref[:, cols].astype(BF16))
        hid = (g * jax.nn.sigmoid(g)) * _dot(x, wu_ref[:, cols].astype(BF16))
        y = y + _dot(hid.astype(BF16), wd_ref[cols, :].astype(BF16))
    y = y * gate_ref[...].reshape(bt * cap, 1)
    y_out[...] = y.astype(BF16).reshape(bt, cap, d)


def _ffn(xsel, gate, wg, wu, wd):
    b, e, cap, d = xsel.shape
    bt = max(1, min(b, MOE_ROWS // cap))
    f = wg.shape[-1]
    return pl.pallas_call(
        _ffn_kernel,
        grid=(e, b // bt),
        in_specs=[pl.BlockSpec((bt, None, cap, d), lambda i, j: (j, i, 0, 0)),
                  pl.BlockSpec((bt, None, cap, 1), lambda i, j: (j, i, 0, 0)),
                  pl.BlockSpec((None, d, f), lambda i, j: (i, 0, 0)),
                  pl.BlockSpec((None, d, f), lambda i, j: (i, 0, 0)),
                  pl.BlockSpec((None, f, d), lambda i, j: (i, 0, 0))],
        out_specs=pl.BlockSpec((bt, None, cap, d), lambda i, j: (j, i, 0, 0)),
        out_shape=jax.ShapeDtypeStruct((b, e, cap, d), BF16),
        compiler_params=_params(2),
        name="ffn",
    )(xsel, gate, wg, wu, wd)


def _combine_kernel(off_ref, x1_ref, y_ref, post_ref, mod_ref, gf_ref, out_ref, ycat_scr, *, cap, wn, n_tiles):
    tn, d = x1_ref.shape
    width = N_EXPERTS * wn
    post = post_ref[...].astype(F32).astype(BF16)
    lane = lax.broadcasted_iota(jnp.int32, (N_EXPERTS, width), 1)
    erow = lax.broadcasted_iota(jnp.int32, (N_EXPERTS, width), 0)
    ids = _dot(post, jnp.where(erow == lane // wn, 1.0, 0.0).astype(BF16))
    lane1 = lax.broadcasted_iota(jnp.int32, (1, width), 1)
    starts, rounds = _tile_windows(off_ref, pl.program_id(0), pl.program_id(1), n_tiles, wn)

    def one_round(r, acc):
        sid = jnp.full((1, width), -2, jnp.int32)
        for e in range(N_EXPERTS):
            lo, base = _window(starts[e], r, wn, cap)
            ycat_scr[e * wn:(e + 1) * wn, :] = y_ref[e, pl.ds(base, wn), :]
            s = base + (lane1 - e * wn)
            sid = jnp.where((lane1 // wn == e) & (s >= lo) & (s < lo + wn), s, sid)
        hit = ids == sid.astype(F32)
        return acc + _dot(jnp.where(hit, 1.0, 0.0).astype(BF16), ycat_scr[...])

    acc = lax.fori_loop(0, rounds, one_round, jnp.zeros((tn, d), F32))
    gate2 = mod_ref[...][5:6]
    out_ref[...] = _rms_rows(x1_ref[...] + gate2 * acc, gf_ref[...])


def _combine(x1, ysel, pos_t, off, mod, gf, cap):
    b, n, d = x1.shape
    tn = MOE_TOKEN_TILE
    wn = _window_rows(cap, n)
    shared = mod.shape[0] == 1
    mod_idx = (lambda i, j, off: (0, 0, 0)) if shared else (lambda i, j, off: (i, 0, 0))
    grid_spec = pltpu.PrefetchScalarGridSpec(
        num_scalar_prefetch=1,
        grid=(b, n // tn),
        in_specs=[pl.BlockSpec((None, tn, d), lambda i, j, off: (i, j, 0)),
                  pl.BlockSpec((None, N_EXPERTS, cap, d), lambda i, j, off: (i, 0, 0, 0)),
                  pl.BlockSpec((None, tn, N_EXPERTS), lambda i, j, off: (i, j, 0)),
                  pl.BlockSpec((None, N_MOD, d), mod_idx),
                  pl.BlockSpec((1, d), lambda i, j, off: (0, 0))],
        out_specs=pl.BlockSpec((None, tn, d), lambda i, j, off: (i, j, 0)),
        scratch_shapes=[pltpu.VMEM((N_EXPERTS * wn, d), BF16)])
    return pl.pallas_call(
        functools.partial(_combine_kernel, cap=cap, wn=wn, n_tiles=n // tn),
        grid_spec=grid_spec,
        out_shape=jax.ShapeDtypeStruct((b, n, d), F32),
        compiler_params=_params(2),
        name="combine",
    )(off, x1, ysel, pos_t, mod, gf)


def _rope_tables(n_tokens):
    pairs = HEAD_DIM // 4
    rows = n_tokens // GRID_W
    row = jnp.repeat(jnp.arange(rows, dtype=F32), GRID_W)
    col = jnp.tile(jnp.arange(GRID_W, dtype=F32), rows)
    inv_freq = ROPE_THETA ** (-jnp.arange(pairs, dtype=F32) / pairs)
    ang = jnp.concatenate([row[:, None] * inv_freq, col[:, None] * inv_freq], axis=-1)
    reps = LANES // (HEAD_DIM // 2)
    return jnp.tile(jnp.cos(ang), (1, reps)), jnp.tile(jnp.sin(ang), (1, reps))


def _trunk(x, mod, w, rope_tabs, ctx_k, ctx_v, ctx_state):
    b, n, d = x.shape
    q, k, v, u = _inproj(x, mod, w["g1"], w["w_in"], w["qg"], w["kg"], w["bd"], rope_tabs)
    kv = [(k, v)]
    if ctx_k is not None:
        kv.append((ctx_k.reshape(b, -1, KV_WIDTH), ctx_v.reshape(b, -1, KV_WIDTH)))
    attn = _attention(q, kv)
    y_ssm, state = _s5_scan(u, w["s5ops"], ctx_state)
    x1, h2, aff_t = _post(x, attn, y_ssm, u, w["d_skip"], w["w_glu"], w["b_glu"], w["w_out"], mod,
                          w["g2"], w["w_router_t"])
    cap = CAPACITY_FACTOR * n // N_EXPERTS
    pos, off = _route(aff_t, cap)
    xsel, gate = _gather(h2, pos, aff_t, off, cap)
    ysel = _ffn(xsel, gate, w["w_gate"], w["w_up"], w["w_down"])
    out = _combine(x1, ysel, pos.transpose(0, 2, 1), off, mod, w["gf"], cap)
    return out, k, v, state


def kernel(x_prompt, x_sample, cache_k, cache_v, state_ssm, c, c_ctx, norm1_g, norm2_g, w_mod, b_mod, w_in, q_norm_g, k_norm_g, ssm_a_re, ssm_a_im, ssm_log_dt, ssm_b_re, ssm_b_im, ssm_c_re, ssm_c_im, ssm_d, w_glu, b_glu, w_out, w_router, w_gate, w_up, w_down, final_norm_g):
    assert norm1_g.shape[0] == 1, "single trunk layer"
    d = x_prompt.shape[-1]
    n_dec = c.shape[0]
    cond = jnp.concatenate([c_ctx[None, :], c, jnp.zeros((16 - 1 - n_dec, d), F32)], axis=0)
    mod = _modulation(cond, w_mod[0], b_mod[0]).reshape(16, N_MOD, d)
    head = jnp.arange(REP * HEAD_DIM) // HEAD_DIM
    w = {
        "g1": norm1_g, "g2": norm2_g, "gf": final_norm_g.reshape(1, d),
        "w_in": w_in[0].astype(BF16),
        "qg": jnp.tile(q_norm_g, (1, REP)),
        "kg": jnp.tile(k_norm_g, (1, REP)),
        "bd": (head[:, None] == head[None, :]).astype(BF16),
        "s5ops": _s5_prep(ssm_a_re[0], ssm_a_im[0], ssm_log_dt[0], ssm_b_re[0], ssm_b_im[0],
                          ssm_c_re[0], ssm_c_im[0]),
        "d_skip": ssm_d, "w_glu": w_glu[0].astype(BF16), "b_glu": b_glu,
        "w_out": w_out[0].astype(BF16),
        "w_router_t": w_router[0].T.astype(BF16),
        "w_gate": w_gate[0], "w_up": w_up[0], "w_down": w_down[0],
    }
    y_p, k_ctx, v_ctx, st_ctx = _trunk(x_prompt, mod[0:1], w, None, None, None, None)
    y_s, _, _, _ = _trunk(x_sample, mod[1:1 + n_dec], w, _rope_tables(x_sample.shape[1]),
                          cache_k[:, 0], cache_v[:, 0], state_ssm[:, 0])
    bp, np_ = x_prompt.shape[:2]
    new_k = k_ctx.reshape(bp, 1, np_, N_KV_HEADS, HEAD_DIM)
    new_v = v_ctx.reshape(bp, 1, np_, N_KV_HEADS, HEAD_DIM)
    return (y_p, y_s, new_k, new_v, st_ctx[:, None])
```

```python
import functools
import math

import jax
import jax.numpy as jnp
from jax import lax
from jax.experimental import pallas as pl
from jax.experimental.pallas import tpu as pltpu

F32 = jnp.float32
BF16 = jnp.bfloat16

HEAD_DIM = 64
N_HEADS = 8
N_KV_HEADS = 2
REP = N_HEADS // N_KV_HEADS
ATTN_WIDTH = N_HEADS * HEAD_DIM
KV_WIDTH = N_KV_HEADS * HEAD_DIM
SSM_GROUP = 16
SSM_STATE = 64
N_EXPERTS = 16
CAPACITY_FACTOR = 2
GRID_W = 64
ROPE_THETA = 10000.0
NORM_EPS = 1e-6
N_MOD = 6
CHUNK = 16
LANES = 128
VMEM_LIMIT = 56 * 1024 * 1024
MOE_ROWS = 1024
TOKEN_ROWS = 1024


def _params(n_grid_axes, vmem=VMEM_LIMIT):
    return pltpu.CompilerParams(dimension_semantics=("arbitrary",) * n_grid_axes,
                                vmem_limit_bytes=vmem)


def _chunked_spec(width, tm):
    return pl.BlockSpec((width // LANES, tm // CHUNK, None, CHUNK, LANES), lambda i, j: (0, j, i, 0, 0))


def _dot(a, b):
    return jnp.dot(a, b, preferred_element_type=F32)


def _dot_nt(a, b):
    return lax.dot_general(a, b, (((1,), (1,)), ((), ())), preferred_element_type=F32)


def _rms_rows(x, g):
    return x * lax.rsqrt(jnp.mean(x * x, axis=-1, keepdims=True) + NORM_EPS) * g


def _mod_kernel(cond_ref, w_ref, b_ref, out_ref):
    c = cond_ref[...]
    s = (c * jax.nn.sigmoid(c)).astype(BF16)
    out_ref[...] = _dot(s, w_ref[...].astype(BF16)) + b_ref[...]


def _modulation(cond, w_mod, b_mod):
    rows, d = cond.shape
    n = w_mod.shape[1]
    tn = 1024
    return pl.pallas_call(
        _mod_kernel,
        grid=(n // tn,),
        in_specs=[pl.BlockSpec((rows, d), lambda j: (0, 0)),
                  pl.BlockSpec((d, tn), lambda j: (0, j)),
                  pl.BlockSpec((1, tn), lambda j: (0, j))],
        out_specs=pl.BlockSpec((rows, tn), lambda j: (0, j)),
        out_shape=jax.ShapeDtypeStruct((rows, n), F32),
        compiler_params=_params(1),
        name="mod",
    )(cond, w_mod, b_mod.reshape(1, n))


def _inproj_kernel(*refs, rope):
    if rope:
        (x_ref, mod_ref, g1_ref, win_ref, qg_ref, kg_ref, bd_ref, cos_ref, sin_ref,
         q_out, k_out, v_out, u_out) = refs
    else:
        (x_ref, mod_ref, g1_ref, win_ref, qg_ref, kg_ref, bd_ref,
         q_out, k_out, v_out, u_out) = refs
    x = x_ref[...]
    m = mod_ref[...]
    shift1, scale1 = m[0:1], m[1:2]
    h = _rms_rows(x, g1_ref[...]) * (1.0 + scale1) + shift1
    z = _dot(h.astype(BF16), win_ref[...])
    bd = bd_ref[...]
    if rope:
        cos = cos_ref[...]
        sin = sin_ref[...]
        lane = lax.broadcasted_iota(jnp.int32, cos.shape, 1)
        first_half = (lane % HEAD_DIM) < (HEAD_DIM // 2)

    def head_norm(zc, g):
        w = zc.shape[1]
        ss = _dot((zc * zc).astype(BF16), bd[:w, :w])
        return zc * lax.rsqrt(ss * (1.0 / HEAD_DIM) + NORM_EPS) * g[:, :w]

    def rope_lanes(zn):
        if not rope:
            return zn
        rot = jnp.where(first_half,
                        -pltpu.roll(zn, LANES - HEAD_DIM // 2, 1),
                        pltpu.roll(zn, HEAD_DIM // 2, 1))
        return zn * cos + rot * sin

    q_scale = HEAD_DIM ** -0.5 * math.log2(math.e)
    nw = bd.shape[0]
    for i in range(0, ATTN_WIDTH, nw):
        qn = head_norm(z[:, i:i + nw], qg_ref[...])
        for j in range(0, nw, LANES):
            q_out[:, i + j:i + j + LANES] = (rope_lanes(qn[:, j:j + LANES]) * q_scale).astype(BF16)
    k_out[...] = rope_lanes(head_norm(z[:, ATTN_WIDTH:ATTN_WIDTH + KV_WIDTH], kg_ref[...]))
    v_out[...] = z[:, ATTN_WIDTH + KV_WIDTH:ATTN_WIDTH + 2 * KV_WIDTH]
    u0 = ATTN_WIDTH + 2 * KV_WIDTH
    for gb in range(u_out.shape[0]):
        u_out[gb] = z[:, u0 + gb * LANES:u0 + (gb + 1) * LANES].reshape(u_out.shape[1:])


def _inproj(x, mod, g1, w_in_bf, qg, kg, bd, rope_tabs):
    b, n, d = x.shape
    tm = min(n, TOKEN_ROWS)
    ssm_w = w_in_bf.shape[1] - ATTN_WIDTH - 2 * KV_WIDTH
    shared = mod.shape[0] == 1
    mod_idx = (lambda i, j: (0, 0, 0)) if shared else (lambda i, j: (i, 0, 0))
    const2 = lambda i, j: (0, 0)
    in_specs = [pl.BlockSpec((None, tm, d), lambda i, j: (i, j, 0)),
                pl.BlockSpec((None, N_MOD, d), mod_idx),
                pl.BlockSpec((1, d), const2),
                pl.BlockSpec(w_in_bf.shape, const2),
                pl.BlockSpec(qg.shape, const2),
                pl.BlockSpec(kg.shape, const2),
                pl.BlockSpec(bd.shape, const2)]
    args = [x, mod, g1, w_in_bf, qg, kg, bd]
    if rope_tabs is not None:
        in_specs += [pl.BlockSpec((tm, LANES), lambda i, j: (j, 0))] * 2
        args += list(rope_tabs)
    tok = lambda w: pl.BlockSpec((None, tm, w), lambda i, j: (i, j, 0))
    return pl.pallas_call(
        functools.partial(_inproj_kernel, rope=rope_tabs is not None),
        grid=(b, n // tm),
        in_specs=in_specs,
        out_specs=[tok(ATTN_WIDTH), tok(KV_WIDTH), tok(KV_WIDTH), _chunked_spec(ssm_w, tm)],
        out_shape=[jax.ShapeDtypeStruct((b, n, ATTN_WIDTH), BF16),
                   jax.ShapeDtypeStruct((b, n, KV_WIDTH), F32),
                   jax.ShapeDtypeStruct((b, n, KV_WIDTH), F32),
                   jax.ShapeDtypeStruct((ssm_w // LANES, n // CHUNK, b, CHUNK, LANES), F32)],
        compiler_params=_params(2),
        name="inproj",
    )(*args)


KV_FILL_ROWS = 256
ATTN_Q_ROWS = 512
ATTN_KEY_SLAB = 256


def _attn_kernel(*refs, n_src):
    q_ref = refs[0]
    kv_refs = refs[1:1 + 2 * n_src]
    out_ref, k4_scr, vt_scr = refs[1 + 2 * n_src:]
    gw = REP * HEAD_DIM

    @pl.when(pl.program_id(1) == 0)
    def _():
        lane = lax.broadcasted_iota(jnp.int32, (KV_FILL_ROWS, KV_WIDTH), 1)
        first = lane < HEAD_DIM
        row0 = 0
        for k_src, v_src in zip(kv_refs[0::2], kv_refs[1::2]):
            for r in range(0, k_src.shape[0], KV_FILL_ROWS):
                rows = slice(row0 + r, row0 + r + KV_FILL_ROWS)
                x = k_src[r:r + KV_FILL_ROWS, :]
                sw = pltpu.roll(x, HEAD_DIM, 1)
                for g, xx in enumerate((jnp.where(first, x, sw), jnp.where(first, sw, x))):
                    xx = xx.astype(BF16)
                    k4_scr[g, rows, :] = jnp.concatenate([xx, xx], axis=1)
                vt = v_src[r:r + KV_FILL_ROWS, :].T
                for g in range(N_KV_HEADS):
                    vt_scr[g, :, rows] = vt[g * HEAD_DIM:(g + 1) * HEAD_DIM, :].astype(BF16)
            row0 += k_src.shape[0]

    lane_head = lax.broadcasted_iota(jnp.int32, (q_ref.shape[0], gw), 1) // HEAD_DIM

    n_keys = k4_scr.shape[1]
    slab = min(n_keys, ATTN_KEY_SLAB)

    def scores(h):
        g, r = divmod(h, REP)
        qg = q_ref[:, g * gw:(g + 1) * gw]
        qm = jnp.where(lane_head == r, qg, jnp.zeros_like(qg))
        sts = [_dot_nt(k4_scr[g, i:i + slab, :], qm) for i in range(0, n_keys, slab)]
        m = functools.reduce(jnp.maximum, [jnp.max(s, axis=0, keepdims=True) for s in sts])
        return sts, m

    heads = []
    nxt = scores(0)
    for h in range(N_HEADS):
        sts, m = nxt
        if h + 1 < N_HEADS:
            nxt = scores(h + 1)
        l = jnp.zeros_like(m)
        o = jnp.zeros((HEAD_DIM, q_ref.shape[0]), F32)
        for i, st in enumerate(sts):
            pt = jnp.exp2(st - m)
            l = l + jnp.sum(pt, axis=0, keepdims=True)
            o = o + _dot(vt_scr[h // REP, :, i * slab:(i + 1) * slab], pt.astype(BF16))
        heads.append(o / l)
    for i in range(0, N_HEADS, 2):
        pair = jnp.concatenate(heads[i:i + 2], axis=0)
        out_ref[:, i * HEAD_DIM:(i + 2) * HEAD_DIM] = pair.T.astype(BF16)


def _attention(q, kv_sources):
    b, n, _ = q.shape
    s = sum(k.shape[1] for k, _ in kv_sources)
    tq = min(n, ATTN_Q_ROWS)
    gw = REP * HEAD_DIM
    in_specs = [pl.BlockSpec((None, tq, ATTN_WIDTH), lambda i, j: (i, j, 0))]
    args = [q]
    for k, v in kv_sources:
        in_specs += [pl.BlockSpec((None, k.shape[1], KV_WIDTH), lambda i, j: (i, 0, 0))] * 2
        args += [k, v]
    return pl.pallas_call(
        functools.partial(_attn_kernel, n_src=len(kv_sources)),
        grid=(b, n // tq),
        in_specs=in_specs,
        out_specs=pl.BlockSpec((None, tq, ATTN_WIDTH), lambda i, j: (i, j, 0)),
        out_shape=jax.ShapeDtypeStruct((b, n, ATTN_WIDTH), BF16),
        scratch_shapes=[pltpu.VMEM((N_KV_HEADS, s, gw), BF16), pltpu.VMEM((N_KV_HEADS, HEAD_DIM, s), BF16)],
        compiler_params=_params(2),
        name="attn",
    )(*args)


def _cmul(ar, ai, br, bi):
    return ar * br - ai * bi, ar * bi + ai * br


def _cpow(ar, ai, e, shape):
    rr = jnp.ones(shape, F32)
    ri = jnp.zeros(shape, F32)
    pr = jnp.broadcast_to(ar, shape)
    pi = jnp.broadcast_to(ai, shape)
    e = jnp.broadcast_to(e, shape)
    for k in range(CHUNK.bit_length()):
        nr, ni = _cmul(rr, ri, pr, pi)
        bit = ((e >> k) & 1) == 1
        rr = jnp.where(bit, nr, rr)
        ri = jnp.where(bit, ni, ri)
        pr, pi = _cmul(pr, pi, pr, pi)
    return rr, ri


def _zoh(ar, ai, dt):
    mag = jnp.exp(ar * dt)
    abr = mag * jnp.cos(ai * dt)
    abi = mag * jnp.sin(ai * dt)
    den = ar * ar + ai * ai
    nre = abr - 1.0
    nim = abi
    cre = (nre * ar + nim * ai) / den
    cim = (nim * ar - nre * ai) / den
    return abr, abi, cre, cim


def _s5prep_kernel(arr_ref, air_ref, ldt_ref, btr_ref, bti_ref, ctr_ref, cti_ref,
                   arp_ref, aip_ref, ldtp_ref, btpr_ref, btpi_ref,
                   m_out, p_out, q_out, a16r_out, a16i_out):
    P, I, T = SSM_STATE, SSM_GROUP, CHUNK
    W = T * I
    hi = lax.Precision.HIGHEST
    blk = lax.broadcasted_iota(jnp.int32, (1, W), 1) // I
    lane = lax.broadcasted_iota(jnp.int32, (I, W), 1)
    zero_q = jnp.zeros((P, W), F32)
    diag = lax.broadcasted_iota(jnp.int32, (P, P), 0) == lax.broadcasted_iota(jnp.int32, (P, P), 1)
    for gg in range(2):
        kalls = []
        for d in range(2):
            dt = jnp.exp(ldt_ref[d, gg])
            abr_r, abi_r, cre_r, cim_r = _zoh(arr_ref[d, gg], air_ref[d, gg], dt)
            bbr, bbi = _cmul(cre_r, cim_r, btr_ref[d, gg], bti_ref[d, gg])
            abr_c = jnp.sum(jnp.where(diag, abr_r, 0.0), axis=1, keepdims=True)
            abi_c = jnp.sum(jnp.where(diag, abi_r, 0.0), axis=1, keepdims=True)

            e_w = blk if d == 0 else (T - 1) - blk
            pwr, pwi = _cpow(abr_c, abi_c, e_w, (P, W))
            w_re, w_im = _cmul(pwr, pwi, ctr_ref[d, gg], cti_ref[d, gg])
            kalls.append(jnp.dot(bbr, w_re, precision=hi, preferred_element_type=F32)
                         - jnp.dot(bbi, w_im, precision=hi, preferred_element_type=F32))

            q_re, q_im = _cmul(w_re, w_im, jnp.broadcast_to(abr_c, (P, W)), jnp.broadcast_to(abi_c, (P, W)))
            pad = lambda q: jnp.concatenate([q, zero_q] if gg == 0 else [zero_q, q], axis=0)
            q_out[gg, (2 * d) * 2 * P:(2 * d + 1) * 2 * P, :] = pad(q_re).astype(BF16)
            q_out[gg, (2 * d + 1) * 2 * P:(2 * d + 2) * 2 * P, :] = pad(-q_im).astype(BF16)

        for s in range(T):
            fwd = jnp.where(lane >= I * s, pltpu.roll(kalls[0], I * s, 1) if s else kalls[0], 0.0)
            sh = (I * (s + 1)) % W
            bwd = jnp.where(lane < I * (s + 1), pltpu.roll(kalls[1], sh, 1) if sh else kalls[1], 0.0)
            m_out[gg, s * I:(s + 1) * I, :] = (fwd + bwd).astype(BF16)

    srow = lax.broadcasted_iota(jnp.int32, (W, 1), 0) // I
    lane_p = lax.broadcasted_iota(jnp.int32, (W, 2 * P), 1)
    for d in range(2):
        dt = jnp.exp(ldtp_ref[d])
        abr, abi, cre, cim = _zoh(arp_ref[d], aip_ref[d], dt)
        bbr, bbi = _cmul(cre, cim, btpr_ref[d], btpi_ref[d])
        e_p = (T - 1) - srow if d == 0 else srow
        ppr, ppi = _cpow(abr, abi, e_p, (W, 2 * P))
        p_re, p_im = _cmul(ppr, ppi, jnp.concatenate([bbr] * T, axis=0), jnp.concatenate([bbi] * T, axis=0))
        for gg in range(2):
            own = (lane_p < P) if gg == 0 else (lane_p >= P)
            p_out[gg, :, (2 * d) * 2 * P:(2 * d + 1) * 2 * P] = jnp.where(own, p_re, 0.0).astype(BF16)
            p_out[gg, :, (2 * d + 1) * 2 * P:(2 * d + 2) * 2 * P] = jnp.where(own, p_im, 0.0).astype(BF16)
        a16r, a16i = _cpow(abr, abi, jnp.full((1, 2 * P), T, jnp.int32), (1, 2 * P))
        a16r_out[d] = a16r
        a16i_out[d] = a16i


def _s5_prep(a_re, a_im, log_dt, b_re, b_im, c_re, c_im):
    _, G, P = a_re.shape
    I, T = SSM_GROUP, CHUNK
    W = T * I
    row = lambda a: a.reshape(2, G, 1, P)
    bt = lambda b: jnp.swapaxes(b, 2, 3)
    ct = lambda c: jnp.tile(jnp.swapaxes(c, 2, 3), (1, 1, 1, T))
    prow = lambda a: a.reshape(2, G // 2, 1, 2 * P)
    pbt = lambda b: (bt(b).reshape(2, G // 2, 2, I, P).transpose(0, 1, 3, 2, 4).reshape(2, G // 2, I, 2 * P))
    spec = lambda r, c: pl.BlockSpec((2, 2, r, c), lambda j: (0, j, 0, 0))
    pspec = lambda r, c: pl.BlockSpec((2, None, r, c), lambda j: (0, j, 0, 0))
    return pl.pallas_call(
        _s5prep_kernel,
        grid=(G // 2,),
        in_specs=[spec(1, P), spec(1, P), spec(1, 1),
                  spec(I, P), spec(I, P), spec(P, W), spec(P, W),
                  pspec(1, 2 * P), pspec(1, 2 * P), pspec(1, 2 * P), pspec(I, 2 * P), pspec(I, 2 * P)],
        out_specs=[pl.BlockSpec((2, W, W), lambda j: (j, 0, 0)),
                   pl.BlockSpec((2, W, 8 * P), lambda j: (j, 0, 0)),
                   pl.BlockSpec((2, 8 * P, W), lambda j: (j, 0, 0)),
                   pspec(1, 2 * P), pspec(1, 2 * P)],
        out_shape=[jax.ShapeDtypeStruct((G, W, W), BF16),
                   jax.ShapeDtypeStruct((G, W, 8 * P), BF16),
                   jax.ShapeDtypeStruct((G, 8 * P, W), BF16),
                   jax.ShapeDtypeStruct((2, G // 2, 1, 2 * P), F32),
                   jax.ShapeDtypeStruct((2, G // 2, 1, 2 * P), F32)],
        compiler_params=_params(1),
        name="s5prep",
    )(row(a_re), row(a_im), log_dt.reshape(2, G, 1, 1),
      bt(b_re), bt(b_im), ct(c_re), ct(c_im),
      prow(a_re), prow(a_im), prow(jnp.repeat(log_dt, P, axis=-1)), pbt(b_re), pbt(b_im))


RELAY_ROWS = 32
S5_OUT_ROWS = 256


def _block_transpose(xs, masks):
    n = len(xs)
    rolled = []
    for delta in range(n):
        comp = xs[delta]
        for q in range(1, n):
            comp = jnp.where(masks[q], xs[(q + delta) % n], comp)
        rolled.append(pltpu.roll(comp, SSM_GROUP * delta, 1) if delta else comp)
    out = []
    for a in range(n):
        y = rolled[(-a) % n]
        for p in range(1, n):
            y = jnp.where(masks[p], rolled[(p - a) % n], y)
        out.append(y)
    return out


def _s5_kernel(*refs, batch, has_h0):
    if has_h0:
        (u_ref, m_ref, p_ref, q_ref, a16r_ref, a16i_ref, h0_ref,
         y_ref, fin_ref, z_scr, s_scr, yt_scr) = refs
    else:
        (u_ref, m_ref, p_ref, q_ref, a16r_ref, a16i_ref,
         y_ref, fin_ref, z_scr, s_scr, yt_scr) = refs
    I, T, RT, RC = SSM_GROUP, CHUNK, RELAY_ROWS, S5_OUT_ROWS
    gpb = LANES // I
    halves = T * I // LANES
    rows = u_ref.shape[0] // T
    n_chunks = rows // batch
    lane_blk = lax.broadcasted_iota(jnp.int32, (RT, LANES), 1) // I
    masks = [lane_blk == p for p in range(gpb)]

    def to_groups(i, carry):
        r0 = pl.multiple_of(i * RT, RT)
        for h in range(halves):
            xs = [u_ref[pl.ds(r0 * T + p + gpb * h, RT, stride=T), :] for p in range(gpb)]
            for g, zg in enumerate(_block_transpose(xs, masks)):
                z_scr[g, pl.ds(r0, RT), h * LANES:(h + 1) * LANES] = zg.astype(BF16)
        return carry

    lax.fori_loop(0, rows // RT, to_groups, 0)

    npair = gpb // 2
    for j in range(npair):
        s_scr[j] = _dot(z_scr[2 * j], p_ref[2 * j]) + _dot(z_scr[2 * j + 1], p_ref[2 * j + 1])
    part = lambda d, c: slice((2 * d + c) * LANES, (2 * d + c + 1) * LANES)

    for j0 in range(npair):
        chains = [(j0, d) for d in range(2)]
        a = [(jnp.broadcast_to(a16r_ref[d, j], (batch, LANES)),
              jnp.broadcast_to(a16i_ref[d, j], (batch, LANES))) for j, d in chains]
        if has_h0:
            init = tuple(h0_ref[j, d, c] for j, d in chains for c in range(2))
        else:
            init = tuple(jnp.zeros((batch, LANES), F32) for _ in range(2 * len(chains)))

        def step(k, carry):
            out = []
            for i, (j, d) in enumerate(chains):
                hr, hi = carry[2 * i], carry[2 * i + 1]
                c = k if d == 0 else n_chunks - 1 - k
                rs = pl.ds(pl.multiple_of(c * batch, batch), batch)
                sr = s_scr[j, rs, part(d, 0)]
                si = s_scr[j, rs, part(d, 1)]
                s_scr[j, rs, part(d, 0)] = hr
                s_scr[j, rs, part(d, 1)] = hi
                ar, ai = a[i]
                out.append(ar * hr - ai * hi + sr)
                out.append(ar * hi + ai * hr + si)
            return tuple(out)

        fin = lax.fori_loop(0, n_chunks, step, init)
        for i, (j, d) in enumerate(chains):
            fin_ref[j, d, 0] = fin[2 * i]
            fin_ref[j, d, 1] = fin[2 * i + 1]

    def out_tile(i, carry):
        r0 = pl.multiple_of(i * RC, RC)
        rs = pl.ds(r0, RC)
        for g in range(gpb):
            yt_scr[g] = _dot(z_scr[g, rs, :], m_ref[g]) + _dot(s_scr[g // 2, rs, :].astype(BF16), q_ref[g])

        def to_tokens(k, carry2):
            q0 = pl.multiple_of(k * RT, RT)
            for h in range(halves):
                xs = [yt_scr[g, pl.ds(q0, RT), h * LANES:(h + 1) * LANES] for g in range(gpb)]
                for p, yp in enumerate(_block_transpose(xs, masks)):
                    y_ref[pl.ds((r0 + q0) * T + p + gpb * h, RT, stride=T), :] = yp
            return carry2

        lax.fori_loop(0, RC // RT, to_tokens, 0)
        return carry

    lax.fori_loop(0, rows // RC, out_tile, 0)


def _s5_scan(u_blk, ops, h0):
    nb, nc, b, _, _ = u_blk.shape
    gpb = LANES // SSM_GROUP
    npair = gpb // 2
    W = CHUNK * SSM_GROUP
    P = SSM_STATE
    rows = nc * b
    m, p_op, q_op, a16r, a16i = ops
    pspec = pl.BlockSpec((2, npair, 1, 2 * P), lambda j: (0, j, 0, 0))
    act = lambda **kw: pl.BlockSpec((None, rows * CHUNK, LANES), lambda j: (j, 0, 0), **kw)
    st = pl.BlockSpec((None, npair, 2, 2, b, 2 * P), lambda j: (j, 0, 0, 0, 0, 0))
    in_specs = [act(pipeline_mode=pl.Buffered(1)), pl.BlockSpec((gpb, W, W), lambda j: (j, 0, 0)),
                pl.BlockSpec((gpb, W, 8 * P), lambda j: (j, 0, 0)),
                pl.BlockSpec((gpb, 8 * P, W), lambda j: (j, 0, 0)), pspec, pspec]
    args = [u_blk.reshape(nb, rows * CHUNK, LANES), m, p_op, q_op, a16r, a16i]
    if h0 is not None:
        in_specs.append(st)
        args.append(h0.reshape(b, 2, 2, nb, npair, 2, P).transpose(3, 4, 1, 2, 0, 5, 6)
                    .reshape(nb, npair, 2, 2, b, 2 * P))
    y, fin = pl.pallas_call(
        functools.partial(_s5_kernel, batch=b, has_h0=h0 is not None),
        grid=(nb,),
        in_specs=in_specs,
        out_specs=[act(), st],
        out_shape=[jax.ShapeDtypeStruct((nb, rows * CHUNK, LANES), F32),
                   jax.ShapeDtypeStruct((nb, npair, 2, 2, b, 2 * P), F32)],
        scratch_shapes=[pltpu.VMEM((gpb, rows, W), BF16),
                        pltpu.VMEM((npair, rows, 8 * P), F32),
                        pltpu.VMEM((gpb, S5_OUT_ROWS, W), F32)],
        compiler_params=_params(1),
        name="s5",
    )(*args)
    fin = (fin.reshape(nb, npair, 2, 2, b, 2, P).transpose(4, 2, 3, 0, 1, 5, 6)
           .reshape(b, 2, 2, nb * gpb, P))
    return y.reshape(u_blk.shape), fin


def _post_kernel(x_ref, attn_ref, y_ref, u_ref, d_ref, wglu_ref, bglu_ref, wout_ref, mod_ref, g2_ref,
                 wrt_ref, x1_out, h2_out, aff_out):
    m = mod_ref[...]
    gate1, shift2, scale2 = m[2:3], m[3:4], m[4:5]
    tm = x_ref.shape[0]
    tokens = lambda r: jnp.concatenate([r[gb].reshape(tm, LANES) for gb in range(r.shape[0])], axis=1)
    y = tokens(y_ref) + tokens(u_ref) * d_ref[...]
    y = jax.nn.gelu(y)
    y = y * jax.nn.sigmoid(_dot(y.astype(BF16), wglu_ref[...]) + bglu_ref[...])
    aw = attn_ref.shape[1]
    proj = _dot(attn_ref[...], wout_ref[:aw, :]) + _dot(y.astype(BF16), wout_ref[aw:, :])
    x1 = x_ref[...] + gate1 * proj
    x1_out[...] = x1
    h2 = (_rms_rows(x1, g2_ref[...]) * (1.0 + scale2) + shift2).astype(BF16)
    h2_out[...] = h2
    logits = _dot_nt(wrt_ref[...], h2)
    e = jnp.exp(logits - jnp.max(logits, axis=0, keepdims=True))
    aff_out[...] = e / jnp.sum(e, axis=0, keepdims=True)


def _post(x, attn, y_ssm, u, d_skip, wglu_bf, b_glu, wout_bf, mod, g2, wrt_bf):
    b, n, d = x.shape
    tm = min(n, TOKEN_ROWS)
    sw = u.shape[0] * LANES
    shared = mod.shape[0] == 1
    mod_idx = (lambda i, j: (0, 0, 0)) if shared else (lambda i, j: (i, 0, 0))
    const2 = lambda i, j: (0, 0)
    tok = lambda w: pl.BlockSpec((None, tm, w), lambda i, j: (i, j, 0))
    return pl.pallas_call(
        _post_kernel,
        grid=(b, n // tm),
        in_specs=[tok(d), tok(attn.shape[-1]), _chunked_spec(sw, tm), _chunked_spec(sw, tm),
                  pl.BlockSpec((1, sw), const2),
                  pl.BlockSpec(wglu_bf.shape, const2),
                  pl.BlockSpec((1, sw), const2),
                  pl.BlockSpec(wout_bf.shape, const2),
                  pl.BlockSpec((None, N_MOD, d), mod_idx),
                  pl.BlockSpec((1, d), const2),
                  pl.BlockSpec(wrt_bf.shape, const2)],
        out_specs=[tok(d), tok(d), pl.BlockSpec((None, N_EXPERTS, tm), lambda i, j: (i, 0, j))],
        out_shape=[jax.ShapeDtypeStruct((b, n, d), F32),
                   jax.ShapeDtypeStruct((b, n, d), BF16),
                   jax.ShapeDtypeStruct((b, N_EXPERTS, n), F32)],
        compiler_params=_params(2),
        name="post",
    )(x, attn, y_ssm, u, d_skip, wglu_bf, b_glu, wout_bf, mod, g2, wrt_bf)


MOE_TOKEN_TILE = 256


def _route_kernel(aff_ref, pos_out, off_out, *, cap):
    a = aff_ref[...]
    rows, n = a.shape
    capf = jnp.float32(cap)

    def count(mask):
        return jnp.sum(jnp.where(mask, 1.0, 0.0), axis=1, keepdims=True)

    def search(i, bits):
        cand = bits | (jnp.int32(1) << (30 - i))
        ok = count(a >= lax.bitcast_convert_type(cand, F32)) >= capf
        return jnp.where(ok, cand, bits)

    thr_bits = lax.fori_loop(0, 31, search, jnp.zeros((rows, 1), jnp.int32))
    thr = lax.bitcast_convert_type(thr_bits, F32)
    gt = a > thr
    eq = a == thr
    before = (lax.broadcasted_iota(jnp.int32, (n, n), 0)
              < lax.broadcasted_iota(jnp.int32, (n, n), 1))
    tri = jnp.where(before, 1.0, 0.0).astype(BF16)
    eq_rank = _dot(jnp.where(eq, 1.0, 0.0).astype(BF16), tri)
    sel = gt | (eq & (eq_rank < capf - count(gt)))
    sel_bf = jnp.where(sel, 1.0, 0.0).astype(BF16)
    pos = _dot(sel_bf, tri)
    pos_out[...] = jnp.where(sel, pos, -1.0).astype(jnp.int32)
    tile_start = lax.broadcasted_iota(jnp.int32, (n, LANES), 1) * MOE_TOKEN_TILE
    in_front = lax.broadcasted_iota(jnp.int32, (n, LANES), 0) < tile_start
    off_out[...] = _dot(sel_bf, jnp.where(in_front, 1.0, 0.0).astype(BF16)).astype(jnp.int32)


def _route(aff_t, cap):
    b, e, n = aff_t.shape
    rows = b * e
    nt = n // MOE_TOKEN_TILE
    pos, off = pl.pallas_call(
        functools.partial(_route_kernel, cap=cap),
        grid=(1,),
        in_specs=[pl.BlockSpec((rows, n), lambda i: (0, 0))],
        out_specs=[pl.BlockSpec((rows, n), lambda i: (0, 0)), pl.BlockSpec((rows, LANES), lambda i: (0, 0))],
        out_shape=[jax.ShapeDtypeStruct((rows, n), jnp.int32), jax.ShapeDtypeStruct((rows, LANES), jnp.int32)],
        compiler_params=_params(1),
        name="route",
    )(aff_t.reshape(rows, n))
    return pos.reshape(b, e, n), off[:, :nt + 1].reshape(-1)


SLOT_ALIGN = 16


def _window_rows(cap, n):
    mean = cap * MOE_TOKEN_TILE // n
    wn = min(cap, max(SLOT_ALIGN, 1 << (2 * mean - 1).bit_length()))
    assert wn & (wn - 1) == 0 and cap % SLOT_ALIGN == 0, "window rows must be a power of two"
    return wn


def _tile_windows(off_ref, request, tile, n_tiles, wn):
    starts, rounds = [], jnp.int32(0)
    for e in range(N_EXPERTS):
        i = (request * N_EXPERTS + e) * (n_tiles + 1) + tile
        start = off_ref[i] & -SLOT_ALIGN
        starts.append(start)
        rounds = jnp.maximum(rounds, lax.shift_right_logical(off_ref[i + 1] - start + (wn - 1),
                                                             jnp.int32(wn.bit_length() - 1)))
    return starts, rounds


def _window(start, r, wn, cap):
    lo = start + r * wn
    base = pl.multiple_of(jnp.minimum(lo, cap - wn), SLOT_ALIGN)
    return lo, base


def _gather_kernel(off_ref, h_ref, pos_ref, aff_ref, x_out, gate_out, p_scr, *, cap, wn):
    n = h_ref.shape[0]
    tt = MOE_TOKEN_TILE
    x_out[...] = jnp.zeros_like(x_out)
    gate_out[...] = jnp.zeros_like(gate_out)
    w_col = lax.broadcasted_iota(jnp.int32, (wn, 1), 0)
    for kt in range(n // tt):
        tile = slice(kt * tt, (kt + 1) * tt)
        starts, rounds = _tile_windows(off_ref, pl.program_id(0), kt, n // tt, wn)

        def one_round(r, carry):
            for e in range(N_EXPERTS):
                lo, base = _window(starts[e], r, wn, cap)
                sid = base + w_col
                sid = jnp.where((sid >= lo) & (sid < lo + wn), sid, -2)
                hit = pos_ref[e:e + 1, tile] == sid
                p_scr[e * wn:(e + 1) * wn, :] = jnp.where(hit, 1.0, 0.0).astype(BF16)
                gate_out[e, pl.ds(base, wn), :] += jnp.sum(jnp.where(hit, aff_ref[e:e + 1, tile], 0.0),
                                                           axis=1, keepdims=True)
            xs = _dot(p_scr[...], h_ref[tile, :]).astype(BF16)
            for e in range(N_EXPERTS):
                _, base = _window(starts[e], r, wn, cap)
                x_out[e, pl.ds(base, wn), :] += xs[e * wn:(e + 1) * wn, :]
            return carry

        lax.fori_loop(0, rounds, one_round, 0)


def _gather(h2, pos, aff_t, off, cap):
    b, n, d = h2.shape
    wn = _window_rows(cap, n)
    grid_spec = pltpu.PrefetchScalarGridSpec(
        num_scalar_prefetch=1,
        grid=(b,),
        in_specs=[pl.BlockSpec((None, n, d), lambda i, off: (i, 0, 0)),
                  pl.BlockSpec((None, N_EXPERTS, n), lambda i, off: (i, 0, 0)),
                  pl.BlockSpec((None, N_EXPERTS, n), lambda i, off: (i, 0, 0))],
        out_specs=[pl.BlockSpec((None, N_EXPERTS, cap, d), lambda i, off: (i, 0, 0, 0)),
                   pl.BlockSpec((None, N_EXPERTS, cap, 1), lambda i, off: (i, 0, 0, 0))],
        scratch_shapes=[pltpu.VMEM((N_EXPERTS * wn, MOE_TOKEN_TILE), BF16)])
    return pl.pallas_call(
        functools.partial(_gather_kernel, cap=cap, wn=wn),
        grid_spec=grid_spec,
        out_shape=[jax.ShapeDtypeStruct((b, N_EXPERTS, cap, d), BF16),
                   jax.ShapeDtypeStruct((b, N_EXPERTS, cap, 1), F32)],
        compiler_params=_params(1),
        name="gather",
    )(off, h2, pos, aff_t)


FFN_SPLIT = 4


def _ffn_kernel(x_ref, gate_ref, wg_ref, wu_ref, wd_ref, y_out):
    bt, cap, d = x_ref.shape
    fs = wg_ref.shape[1] // FFN_SPLIT
    x = x_ref[...].reshape(bt * cap, d)
    y = jnp.zeros((bt * cap, d), F32)
    for h in range(FFN_SPLIT):
        cols = slice(h * fs, (h + 1) * fs)
        g = _dot(x, wg_ref[:, cols].astype(BF16))
        hid = (g * jax.nn.sigmoid(g)) * _dot(x, wu_ref[:, cols].astype(BF16))
        y = y + _dot(hid.astype(BF16), wd_ref[cols, :].astype(BF16))
    y = y * gate_ref[...].reshape(bt * cap, 1)
    y_out[...] = y.astype(BF16).reshape(bt, cap, d)


def _ffn(xsel, gate, wg, wu, wd):
    b, e, cap, d = xsel.shape
    bt = max(1, min(b, MOE_ROWS // cap))
    f = wg.shape[-1]
    return pl.pallas_call(
        _ffn_kernel,
        grid=(e, b // bt),
        in_specs=[pl.BlockSpec((bt, None, cap, d), lambda i, j: (j, i, 0, 0)),
                  pl.BlockSpec((bt, None, cap, 1), lambda i, j: (j, i, 0, 0)),
                  pl.BlockSpec((None, d, f), lambda i, j: (i, 0, 0)),
                  pl.BlockSpec((None, d, f), lambda i, j: (i, 0, 0)),
                  pl.BlockSpec((None, f, d), lambda i, j: (i, 0, 0))],
        out_specs=pl.BlockSpec((bt, None, cap, d), lambda i, j: (j, i, 0, 0)),
        out_shape=jax.ShapeDtypeStruct((b, e, cap, d), BF16),
        compiler_params=_params(2),
        name="ffn",
    )(xsel, gate, wg, wu, wd)


def _combine_kernel(off_ref, x1_ref, y_ref, post_ref, mod_ref, gf_ref, out_ref, ycat_scr, *, cap, wn, n_tiles):
    tn, d = x1_ref.shape
    width = N_EXPERTS * wn
    post = post_ref[...].astype(F32).astype(BF16)
    lane = lax.broadcasted_iota(jnp.int32, (N_EXPERTS, width), 1)
    erow = lax.broadcasted_iota(jnp.int32, (N_EXPERTS, width), 0)
    expand = jnp.where(erow == lane // wn, 1.0, 0.0).astype(BF16)
    lane1 = lax.broadcasted_iota(jnp.int32, (1, width), 1)
    starts, rounds = _tile_windows(off_ref, pl.program_id(0), pl.program_id(1), n_tiles, wn)

    def one_round(r):
        sid = jnp.full((1, width), -2, jnp.int32)
        for e in range(N_EXPERTS):
            lo, base = _window(starts[e], r, wn, cap)
            ycat_scr[e * wn:(e + 1) * wn, :] = y_ref[e, pl.ds(base, wn), :]
            s = base + (lane1 - e * wn)
            sid = jnp.where((lane1 // wn == e) & (s >= lo) & (s < lo + wn), s, sid)
        hit = _dot(post, expand) == sid.astype(F32)
        return _dot(jnp.where(hit, 1.0, 0.0).astype(BF16), ycat_scr[...])

    acc = lax.fori_loop(1, rounds, lambda r, acc: acc + one_round(r), one_round(0))
    gate2 = mod_ref[...][5:6]
    out_ref[...] = _rms_rows(x1_ref[...] + gate2 * acc, gf_ref[...])


def _combine(x1, ysel, pos_t, off, mod, gf, cap):
    b, n, d = x1.shape
    tn = MOE_TOKEN_TILE
    wn = _window_rows(cap, n)
    shared = mod.shape[0] == 1
    mod_idx = (lambda i, j, off: (0, 0, 0)) if shared else (lambda i, j, off: (i, 0, 0))
    grid_spec = pltpu.PrefetchScalarGridSpec(
        num_scalar_prefetch=1,
        grid=(b, n // tn),
        in_specs=[pl.BlockSpec((None, tn, d), lambda i, j, off: (i, j, 0)),
                  pl.BlockSpec((None, N_EXPERTS, cap, d), lambda i, j, off: (i, 0, 0, 0)),
                  pl.BlockSpec((None, tn, N_EXPERTS), lambda i, j, off: (i, j, 0)),
                  pl.BlockSpec((None, N_MOD, d), mod_idx),
                  pl.BlockSpec((1, d), lambda i, j, off: (0, 0))],
        out_specs=pl.BlockSpec((None, tn, d), lambda i, j, off: (i, j, 0)),
        scratch_shapes=[pltpu.VMEM((N_EXPERTS * wn, d), BF16)])
    return pl.pallas_call(
        functools.partial(_combine_kernel, cap=cap, wn=wn, n_tiles=n // tn),
        grid_spec=grid_spec,
        out_shape=jax.ShapeDtypeStruct((b, n, d), F32),
        compiler_params=_params(2),
        name="combine",
    )(off, x1, ysel, pos_t, mod, gf)


def _rope_tables(n_tokens):
    pairs = HEAD_DIM // 4
    rows = n_tokens // GRID_W
    row = jnp.repeat(jnp.arange(rows, dtype=F32), GRID_W)
    col = jnp.tile(jnp.arange(GRID_W, dtype=F32), rows)
    inv_freq = ROPE_THETA ** (-jnp.arange(pairs, dtype=F32) / pairs)
    ang = jnp.concatenate([row[:, None] * inv_freq, col[:, None] * inv_freq], axis=-1)
    reps = LANES // (HEAD_DIM // 2)
    return jnp.tile(jnp.cos(ang), (1, reps)), jnp.tile(jnp.sin(ang), (1, reps))


def _trunk(x, mod, w, rope_tabs, ctx_k, ctx_v, ctx_state):
    b, n, d = x.shape
    q, k, v, u = _inproj(x, mod, w["g1"], w["w_in"], w["qg"], w["kg"], w["bd"], rope_tabs)
    kv = [(k, v)]
    if ctx_k is not None:
        kv.append((ctx_k.reshape(b, -1, KV_WIDTH), ctx_v.reshape(b, -1, KV_WIDTH)))
    attn = _attention(q, kv)
    y_ssm, state = _s5_scan(u, w["s5ops"], ctx_state)
    x1, h2, aff_t = _post(x, attn, y_ssm, u, w["d_skip"], w["w_glu"], w["b_glu"], w["w_out"], mod,
                          w["g2"], w["w_router_t"])
    cap = CAPACITY_FACTOR * n // N_EXPERTS
    pos, off = _route(aff_t, cap)
    xsel, gate = _gather(h2, pos, aff_t, off, cap)
    ysel = _ffn(xsel, gate, w["w_gate"], w["w_up"], w["w_down"])
    out = _combine(x1, ysel, pos.transpose(0, 2, 1), off, mod, w["gf"], cap)
    return out, k, v, state


def kernel(x_prompt, x_sample, cache_k, cache_v, state_ssm, c, c_ctx, norm1_g, norm2_g, w_mod, b_mod, w_in, q_norm_g, k_norm_g, ssm_a_re, ssm_a_im, ssm_log_dt, ssm_b_re, ssm_b_im, ssm_c_re, ssm_c_im, ssm_d, w_glu, b_glu, w_out, w_router, w_gate, w_up, w_down, final_norm_g):
    assert norm1_g.shape[0] == 1, "single trunk layer"
    d = x_prompt.shape[-1]
    n_dec = c.shape[0]
    cond = jnp.concatenate([c_ctx[None, :], c, jnp.zeros((16 - 1 - n_dec, d), F32)], axis=0)
    mod = _modulation(cond, w_mod[0], b_mod[0]).reshape(16, N_MOD, d)
    head = jnp.arange(REP * HEAD_DIM) // HEAD_DIM
    w = {
        "g1": norm1_g, "g2": norm2_g, "gf": final_norm_g.reshape(1, d),
        "w_in": w_in[0].astype(BF16),
        "qg": jnp.tile(q_norm_g, (1, REP)),
        "kg": jnp.tile(k_norm_g, (1, REP)),
        "bd": (head[:, None] == head[None, :]).astype(BF16),
        "s5ops": _s5_prep(ssm_a_re[0], ssm_a_im[0], ssm_log_dt[0], ssm_b_re[0], ssm_b_im[0],
                          ssm_c_re[0], ssm_c_im[0]),
        "d_skip": ssm_d, "w_glu": w_glu[0].astype(BF16), "b_glu": b_glu,
        "w_out": w_out[0].astype(BF16),
        "w_router_t": w_router[0].T.astype(BF16),
        "w_gate": w_gate[0], "w_up": w_up[0], "w_down": w_down[0],
    }
    y_p, k_ctx, v_ctx, st_ctx = _trunk(x_prompt, mod[0:1], w, None, None, None, None)
    y_s, _, _, _ = _trunk(x_sample, mod[1:1 + n_dec], w, _rope_tables(x_sample.shape[1]),
                          cache_k[:, 0], cache_v[:, 0], state_ssm[:, 0])
    bp, np_ = x_prompt.shape[:2]
    new_k = k_ctx.reshape(bp, 1, np_, N_KV_HEADS, HEAD_DIM)
    new_v = v_ctx.reshape(bp, 1, np_, N_KV_HEADS, HEAD_DIM)
    return (y_p, y_s, new_k, new_v, st_ctx[:, None])
```

```python
import functools
import math

import jax
import jax.numpy as jnp
from jax import lax
from jax.experimental import pallas as pl
from jax.experimental.pallas import tpu as pltpu

F32 = jnp.float32
BF16 = jnp.bfloat16

HEAD_DIM = 64
N_HEADS = 8
N_KV_HEADS = 2
REP = N_HEADS // N_KV_HEADS
ATTN_WIDTH = N_HEADS * HEAD_DIM
KV_WIDTH = N_KV_HEADS * HEAD_DIM
SSM_GROUP = 16
SSM_STATE = 64
N_EXPERTS = 16
CAPACITY_FACTOR = 2
GRID_W = 64
ROPE_THETA = 10000.0
NORM_EPS = 1e-6
N_MOD = 6
CHUNK = 16
LANES = 128
SUBLANES = 8
VMEM_LIMIT = 56 * 1024 * 1024
MOE_ROWS = 1024
TOKEN_ROWS = 1024


def _params(n_grid_axes, vmem=VMEM_LIMIT):
    return pltpu.CompilerParams(dimension_semantics=("arbitrary",) * n_grid_axes,
                                vmem_limit_bytes=vmem)


def _chunked_spec(width, tm):
    return pl.BlockSpec((width // LANES, tm // CHUNK, None, CHUNK, LANES), lambda i, j: (0, j, i, 0, 0))


def _dot(a, b):
    return jnp.dot(a, b, preferred_element_type=F32)


def _dot_nt(a, b):
    return lax.dot_general(a, b, (((1,), (1,)), ((), ())), preferred_element_type=F32)


def _rms_rows(x, g):
    return x * lax.rsqrt(jnp.mean(x * x, axis=-1, keepdims=True) + NORM_EPS) * g


def _mod_kernel(cond_ref, w_ref, b_ref, out_ref):
    c = cond_ref[...]
    s = (c * jax.nn.sigmoid(c)).astype(BF16)
    out_ref[...] = _dot(s, w_ref[...].astype(BF16)) + b_ref[...]


def _modulation(cond, w_mod, b_mod):
    rows, d = cond.shape
    n = w_mod.shape[1]
    tn = TOKEN_ROWS
    assert n % tn == 0
    return pl.pallas_call(
        _mod_kernel,
        grid=(n // tn,),
        in_specs=[pl.BlockSpec((rows, d), lambda j: (0, 0)),
                  pl.BlockSpec((d, tn), lambda j: (0, j)),
                  pl.BlockSpec((1, tn), lambda j: (0, j))],
        out_specs=pl.BlockSpec((rows, tn), lambda j: (0, j)),
        out_shape=jax.ShapeDtypeStruct((rows, n), F32),
        compiler_params=_params(1),
        name="mod",
    )(cond, w_mod, b_mod.reshape(1, n))


def _inproj_kernel(*refs, rope):
    if rope:
        (x_ref, mod_ref, g1_ref, win_ref, qg_ref, kg_ref, bd_ref, cos_ref, sin_ref,
         q_out, k_out, v_out, u_out) = refs
    else:
        (x_ref, mod_ref, g1_ref, win_ref, qg_ref, kg_ref, bd_ref,
         q_out, k_out, v_out, u_out) = refs
    x = x_ref[...]
    m = mod_ref[...]
    shift1, scale1 = m[0:1], m[1:2]
    h = _rms_rows(x, g1_ref[...]) * (1.0 + scale1) + shift1
    z = _dot(h.astype(BF16), win_ref[...])
    bd = bd_ref[...]
    if rope:
        cos = cos_ref[...]
        sin = sin_ref[...]
        lane = lax.broadcasted_iota(jnp.int32, cos.shape, 1)
        first_half = (lane % HEAD_DIM) < (HEAD_DIM // 2)

    def head_norm(zc, g):
        w = zc.shape[1]
        ss = _dot((zc * zc).astype(BF16), bd[:w, :w])
        return zc * lax.rsqrt(ss * (1.0 / HEAD_DIM) + NORM_EPS) * g[:, :w]

    def rope_lanes(zn):
        if not rope:
            return zn
        rot = jnp.where(first_half,
                        -pltpu.roll(zn, LANES - HEAD_DIM // 2, 1),
                        pltpu.roll(zn, HEAD_DIM // 2, 1))
        return zn * cos + rot * sin

    q_scale = HEAD_DIM ** -0.5 * math.log2(math.e)
    nw = bd.shape[0]
    for i in range(0, ATTN_WIDTH, nw):
        qn = head_norm(z[:, i:i + nw], qg_ref[...])
        for j in range(0, nw, LANES):
            q_out[:, i + j:i + j + LANES] = (rope_lanes(qn[:, j:j + LANES]) * q_scale).astype(BF16)
    k_out[...] = rope_lanes(head_norm(z[:, ATTN_WIDTH:ATTN_WIDTH + KV_WIDTH], kg_ref[...]))
    v_out[...] = z[:, ATTN_WIDTH + KV_WIDTH:ATTN_WIDTH + 2 * KV_WIDTH]
    u0 = ATTN_WIDTH + 2 * KV_WIDTH
    for gb in range(u_out.shape[0]):
        u_out[gb] = z[:, u0 + gb * LANES:u0 + (gb + 1) * LANES].reshape(u_out.shape[1:])


def _inproj(x, mod, g1, w_in_bf, qg, kg, bd, rope_tabs):
    b, n, d = x.shape
    tm = min(n, TOKEN_ROWS)
    ssm_w = w_in_bf.shape[1] - ATTN_WIDTH - 2 * KV_WIDTH
    shared = mod.shape[0] == 1
    mod_idx = (lambda i, j: (0, 0, 0)) if shared else (lambda i, j: (i, 0, 0))
    const2 = lambda i, j: (0, 0)
    in_specs = [pl.BlockSpec((None, tm, d), lambda i, j: (i, j, 0)),
                pl.BlockSpec((None, N_MOD, d), mod_idx),
                pl.BlockSpec((1, d), const2),
                pl.BlockSpec(w_in_bf.shape, const2),
                pl.BlockSpec(qg.shape, const2),
                pl.BlockSpec(kg.shape, const2),
                pl.BlockSpec(bd.shape, const2)]
    args = [x, mod, g1, w_in_bf, qg, kg, bd]
    if rope_tabs is not None:
        in_specs += [pl.BlockSpec((tm, LANES), lambda i, j: (j, 0))] * 2
        args += list(rope_tabs)
    tok = lambda w: pl.BlockSpec((None, tm, w), lambda i, j: (i, j, 0))
    return pl.pallas_call(
        functools.partial(_inproj_kernel, rope=rope_tabs is not None),
        grid=(b, n // tm),
        in_specs=in_specs,
        out_specs=[tok(ATTN_WIDTH), tok(KV_WIDTH), tok(KV_WIDTH), _chunked_spec(ssm_w, tm)],
        out_shape=[jax.ShapeDtypeStruct((b, n, ATTN_WIDTH), BF16),
                   jax.ShapeDtypeStruct((b, n, KV_WIDTH), F32),
                   jax.ShapeDtypeStruct((b, n, KV_WIDTH), F32),
                   jax.ShapeDtypeStruct((ssm_w // LANES, n // CHUNK, b, CHUNK, LANES), F32)],
        compiler_params=_params(2),
        name="inproj",
    )(*args)


KV_FILL_ROWS = 256
ATTN_Q_ROWS = 512
ATTN_KEY_SLAB = 256


def _attn_kernel(*refs, n_src):
    q_ref = refs[0]
    kv_refs = refs[1:1 + 2 * n_src]
    out_ref, k4_scr, vt_scr = refs[1 + 2 * n_src:]
    gw = REP * HEAD_DIM

    @pl.when(pl.program_id(1) == 0)
    def _():
        lane = lax.broadcasted_iota(jnp.int32, (KV_FILL_ROWS, KV_WIDTH), 1)
        first = lane < HEAD_DIM
        row0 = 0
        for k_src, v_src in zip(kv_refs[0::2], kv_refs[1::2]):
            for r in range(0, k_src.shape[0], KV_FILL_ROWS):
                rows = slice(row0 + r, row0 + r + KV_FILL_ROWS)
                x = k_src[r:r + KV_FILL_ROWS, :]
                sw = pltpu.roll(x, HEAD_DIM, 1)
                for g, xx in enumerate((jnp.where(first, x, sw), jnp.where(first, sw, x))):
                    xx = xx.astype(BF16)
                    k4_scr[g, rows, :] = jnp.concatenate([xx, xx], axis=1)
                vt = v_src[r:r + KV_FILL_ROWS, :].T
                for g in range(N_KV_HEADS):
                    vt_scr[g, :, rows] = vt[g * HEAD_DIM:(g + 1) * HEAD_DIM, :].astype(BF16)
            row0 += k_src.shape[0]

    lane_head = lax.broadcasted_iota(jnp.int32, (q_ref.shape[0], gw), 1) // HEAD_DIM

    n_keys = k4_scr.shape[1]
    slab = min(n_keys, ATTN_KEY_SLAB)

    def scores(h):
        g, r = divmod(h, REP)
        qg = q_ref[:, g * gw:(g + 1) * gw]
        qm = jnp.where(lane_head == r, qg, jnp.zeros_like(qg))
        sts = [_dot_nt(k4_scr[g, i:i + slab, :], qm) for i in range(0, n_keys, slab)]
        m = functools.reduce(jnp.maximum, [jnp.max(s, axis=0, keepdims=True) for s in sts])
        return sts, m

    heads = []
    nxt = scores(0)
    for h in range(N_HEADS):
        sts, m = nxt
        if h + 1 < N_HEADS:
            nxt = scores(h + 1)
        l = jnp.zeros_like(m)
        o = jnp.zeros((HEAD_DIM, q_ref.shape[0]), F32)
        for i, st in enumerate(sts):
            pt = jnp.exp2(st - m)
            l = l + jnp.sum(pt, axis=0, keepdims=True)
            o = o + _dot(vt_scr[h // REP, :, i * slab:(i + 1) * slab], pt.astype(BF16))
        heads.append(o / l)
    for i in range(0, N_HEADS, 2):
        pair = jnp.concatenate(heads[i:i + 2], axis=0)
        out_ref[:, i * HEAD_DIM:(i + 2) * HEAD_DIM] = pair.T.astype(BF16)


def _attention(q, kv_sources):
    b, n, _ = q.shape
    s = sum(k.shape[1] for k, _ in kv_sources)
    tq = min(n, ATTN_Q_ROWS)
    assert n % tq == 0 and all(k.shape[1] % KV_FILL_ROWS == 0 for k, _ in kv_sources)
    gw = REP * HEAD_DIM
    in_specs = [pl.BlockSpec((None, tq, ATTN_WIDTH), lambda i, j: (i, j, 0))]
    args = [q]
    for k, v in kv_sources:
        in_specs += [pl.BlockSpec((None, k.shape[1], KV_WIDTH), lambda i, j: (i, 0, 0))] * 2
        args += [k, v]
    return pl.pallas_call(
        functools.partial(_attn_kernel, n_src=len(kv_sources)),
        grid=(b, n // tq),
        in_specs=in_specs,
        out_specs=pl.BlockSpec((None, tq, ATTN_WIDTH), lambda i, j: (i, j, 0)),
        out_shape=jax.ShapeDtypeStruct((b, n, ATTN_WIDTH), BF16),
        scratch_shapes=[pltpu.VMEM((N_KV_HEADS, s, gw), BF16), pltpu.VMEM((N_KV_HEADS, HEAD_DIM, s), BF16)],
        compiler_params=_params(2),
        name="attn",
    )(*args)


def _cmul(ar, ai, br, bi):
    return ar * br - ai * bi, ar * bi + ai * br


def _cpow(ar, ai, e, shape):
    rr = jnp.ones(shape, F32)
    ri = jnp.zeros(shape, F32)
    pr = jnp.broadcast_to(ar, shape)
    pi = jnp.broadcast_to(ai, shape)
    e = jnp.broadcast_to(e, shape)
    for k in range(CHUNK.bit_length()):
        nr, ni = _cmul(rr, ri, pr, pi)
        bit = ((e >> k) & 1) == 1
        rr = jnp.where(bit, nr, rr)
        ri = jnp.where(bit, ni, ri)
        pr, pi = _cmul(pr, pi, pr, pi)
    return rr, ri


def _zoh(ar, ai, dt):
    mag = jnp.exp(ar * dt)
    abr = mag * jnp.cos(ai * dt)
    abi = mag * jnp.sin(ai * dt)
    den = ar * ar + ai * ai
    nre = abr - 1.0
    nim = abi
    cre = (nre * ar + nim * ai) / den
    cim = (nim * ar - nre * ai) / den
    return abr, abi, cre, cim


def _s5prep_kernel(arr_ref, air_ref, ldt_ref, btr_ref, bti_ref, ctr_ref, cti_ref,
                   arp_ref, aip_ref, ldtp_ref, btpr_ref, btpi_ref,
                   m_out, p_out, q_out, a16r_out, a16i_out):
    P, I, T = SSM_STATE, SSM_GROUP, CHUNK
    W = T * I
    hi = lax.Precision.HIGHEST
    blk = lax.broadcasted_iota(jnp.int32, (1, W), 1) // I
    lane = lax.broadcasted_iota(jnp.int32, (I, W), 1)
    zero_q = jnp.zeros((P, W), F32)
    diag = lax.broadcasted_iota(jnp.int32, (P, P), 0) == lax.broadcasted_iota(jnp.int32, (P, P), 1)
    for gg in range(2):
        kalls = []
        for d in range(2):
            dt = jnp.exp(ldt_ref[d, gg])
            abr_r, abi_r, cre_r, cim_r = _zoh(arr_ref[d, gg], air_ref[d, gg], dt)
            bbr, bbi = _cmul(cre_r, cim_r, btr_ref[d, gg], bti_ref[d, gg])
            abr_c = jnp.sum(jnp.where(diag, abr_r, 0.0), axis=1, keepdims=True)
            abi_c = jnp.sum(jnp.where(diag, abi_r, 0.0), axis=1, keepdims=True)

            e_w = blk if d == 0 else (T - 1) - blk
            pwr, pwi = _cpow(abr_c, abi_c, e_w, (P, W))
            w_re, w_im = _cmul(pwr, pwi, ctr_ref[d, gg], cti_ref[d, gg])
            kalls.append(jnp.dot(bbr, w_re, precision=hi, preferred_element_type=F32)
                         - jnp.dot(bbi, w_im, precision=hi, preferred_element_type=F32))

            q_re, q_im = _cmul(w_re, w_im, jnp.broadcast_to(abr_c, (P, W)), jnp.broadcast_to(abi_c, (P, W)))
            pad = lambda q: jnp.concatenate([q, zero_q] if gg == 0 else [zero_q, q], axis=0)
            q_out[gg, (2 * d) * 2 * P:(2 * d + 1) * 2 * P, :] = pad(q_re).astype(BF16)
            q_out[gg, (2 * d + 1) * 2 * P:(2 * d + 2) * 2 * P, :] = pad(-q_im).astype(BF16)

        for s in range(T):
            fwd = jnp.where(lane >= I * s, pltpu.roll(kalls[0], I * s, 1) if s else kalls[0], 0.0)
            sh = (I * (s + 1)) % W
            bwd = jnp.where(lane < I * (s + 1), pltpu.roll(kalls[1], sh, 1) if sh else kalls[1], 0.0)
            m_out[gg, s * I:(s + 1) * I, :] = (fwd + bwd).astype(BF16)

    srow = lax.broadcasted_iota(jnp.int32, (W, 1), 0) // I
    lane_p = lax.broadcasted_iota(jnp.int32, (W, 2 * P), 1)
    for d in range(2):
        dt = jnp.exp(ldtp_ref[d])
        abr, abi, cre, cim = _zoh(arp_ref[d], aip_ref[d], dt)
        bbr, bbi = _cmul(cre, cim, btpr_ref[d], btpi_ref[d])
        e_p = (T - 1) - srow if d == 0 else srow
        ppr, ppi = _cpow(abr, abi, e_p, (W, 2 * P))
        p_re, p_im = _cmul(ppr, ppi, jnp.concatenate([bbr] * T, axis=0), jnp.concatenate([bbi] * T, axis=0))
        for gg in range(2):
            own = (lane_p < P) if gg == 0 else (lane_p >= P)
            p_out[gg, :, (2 * d) * 2 * P:(2 * d + 1) * 2 * P] = jnp.where(own, p_re, 0.0).astype(BF16)
            p_out[gg, :, (2 * d + 1) * 2 * P:(2 * d + 2) * 2 * P] = jnp.where(own, p_im, 0.0).astype(BF16)
        a16r, a16i = _cpow(abr, abi, jnp.full((1, 2 * P), T, jnp.int32), (1, 2 * P))
        a16r_out[d] = a16r
        a16i_out[d] = a16i


def _s5_prep(a_re, a_im, log_dt, b_re, b_im, c_re, c_im):
    _, G, P = a_re.shape
    I, T = SSM_GROUP, CHUNK
    W = T * I
    row = lambda a: a.reshape(2, G, 1, P)
    bt = lambda b: jnp.swapaxes(b, 2, 3)
    ct = lambda c: jnp.tile(jnp.swapaxes(c, 2, 3), (1, 1, 1, T))
    prow = lambda a: a.reshape(2, G // 2, 1, 2 * P)
    pbt = lambda b: (bt(b).reshape(2, G // 2, 2, I, P).transpose(0, 1, 3, 2, 4).reshape(2, G // 2, I, 2 * P))
    spec = lambda r, c: pl.BlockSpec((2, 2, r, c), lambda j: (0, j, 0, 0))
    pspec = lambda r, c: pl.BlockSpec((2, None, r, c), lambda j: (0, j, 0, 0))
    return pl.pallas_call(
        _s5prep_kernel,
        grid=(G // 2,),
        in_specs=[spec(1, P), spec(1, P), spec(1, 1),
                  spec(I, P), spec(I, P), spec(P, W), spec(P, W),
                  pspec(1, 2 * P), pspec(1, 2 * P), pspec(1, 2 * P), pspec(I, 2 * P), pspec(I, 2 * P)],
        out_specs=[pl.BlockSpec((2, W, W), lambda j: (j, 0, 0)),
                   pl.BlockSpec((2, W, 8 * P), lambda j: (j, 0, 0)),
                   pl.BlockSpec((2, 8 * P, W), lambda j: (j, 0, 0)),
                   pspec(1, 2 * P), pspec(1, 2 * P)],
        out_shape=[jax.ShapeDtypeStruct((G, W, W), BF16),
                   jax.ShapeDtypeStruct((G, W, 8 * P), BF16),
                   jax.ShapeDtypeStruct((G, 8 * P, W), BF16),
                   jax.ShapeDtypeStruct((2, G // 2, 1, 2 * P), F32),
                   jax.ShapeDtypeStruct((2, G // 2, 1, 2 * P), F32)],
        compiler_params=_params(1),
        name="s5prep",
    )(row(a_re), row(a_im), log_dt.reshape(2, G, 1, 1),
      bt(b_re), bt(b_im), ct(c_re), ct(c_im),
      prow(a_re), prow(a_im), prow(jnp.repeat(log_dt, P, axis=-1)), pbt(b_re), pbt(b_im))


RELAY_ROWS = 32
S5_OUT_ROWS = 256


def _block_transpose(xs, masks):
    n = len(xs)
    rolled = []
    for delta in range(n):
        comp = xs[delta]
        for q in range(1, n):
            comp = jnp.where(masks[q], xs[(q + delta) % n], comp)
        rolled.append(pltpu.roll(comp, SSM_GROUP * delta, 1) if delta else comp)
    out = []
    for a in range(n):
        y = rolled[(-a) % n]
        for p in range(1, n):
            y = jnp.where(masks[p], rolled[(p - a) % n], y)
        out.append(y)
    return out


def _s5_kernel(*refs, batch, has_h0):
    if has_h0:
        (u_ref, m_ref, p_ref, q_ref, a16r_ref, a16i_ref, h0_ref,
         y_ref, fin_ref, z_scr, s_scr, yt_scr) = refs
    else:
        (u_ref, m_ref, p_ref, q_ref, a16r_ref, a16i_ref,
         y_ref, fin_ref, z_scr, s_scr, yt_scr) = refs
    I, T, RT, RC = SSM_GROUP, CHUNK, RELAY_ROWS, S5_OUT_ROWS
    gpb = LANES // I
    halves = T * I // LANES
    rows = u_ref.shape[0] // T
    n_chunks = rows // batch
    lane_blk = lax.broadcasted_iota(jnp.int32, (RT, LANES), 1) // I
    masks = [lane_blk == p for p in range(gpb)]

    def to_groups(i, carry):
        r0 = pl.multiple_of(i * RT, RT)
        for h in range(halves):
            xs = [u_ref[pl.ds(r0 * T + p + gpb * h, RT, stride=T), :] for p in range(gpb)]
            for g, zg in enumerate(_block_transpose(xs, masks)):
                z_scr[g, pl.ds(r0, RT), h * LANES:(h + 1) * LANES] = zg.astype(BF16)
        return carry

    lax.fori_loop(0, rows // RT, to_groups, 0)

    npair = gpb // 2
    for j in range(npair):
        s_scr[j] = _dot(z_scr[2 * j], p_ref[2 * j]) + _dot(z_scr[2 * j + 1], p_ref[2 * j + 1])
    part = lambda d, c: slice((2 * d + c) * LANES, (2 * d + c + 1) * LANES)

    for j0 in range(npair):
        chains = [(j0, d) for d in range(2)]
        a = [(jnp.broadcast_to(a16r_ref[d, j], (batch, LANES)),
              jnp.broadcast_to(a16i_ref[d, j], (batch, LANES))) for j, d in chains]
        if has_h0:
            init = tuple(h0_ref[j, d, c] for j, d in chains for c in range(2))
        else:
            init = tuple(jnp.zeros((batch, LANES), F32) for _ in range(2 * len(chains)))

        def step(k, carry):
            out = []
            for i, (j, d) in enumerate(chains):
                hr, hi = carry[2 * i], carry[2 * i + 1]
                c = k if d == 0 else n_chunks - 1 - k
                rs = pl.ds(pl.multiple_of(c * batch, batch), batch)
                sr = s_scr[j, rs, part(d, 0)]
                si = s_scr[j, rs, part(d, 1)]
                s_scr[j, rs, part(d, 0)] = hr
                s_scr[j, rs, part(d, 1)] = hi
                ar, ai = a[i]
                out.append(ar * hr - ai * hi + sr)
                out.append(ar * hi + ai * hr + si)
            return tuple(out)

        fin = lax.fori_loop(0, n_chunks, step, init)
        for i, (j, d) in enumerate(chains):
            fin_ref[j, d, 0] = fin[2 * i]
            fin_ref[j, d, 1] = fin[2 * i + 1]

    def out_tile(i, carry):
        r0 = pl.multiple_of(i * RC, RC)
        rs = pl.ds(r0, RC)
        for g in range(gpb):
            yt_scr[g] = _dot(z_scr[g, rs, :], m_ref[g]) + _dot(s_scr[g // 2, rs, :].astype(BF16), q_ref[g])

        def to_tokens(k, carry2):
            q0 = pl.multiple_of(k * RT, RT)
            for h in range(halves):
                xs = [yt_scr[g, pl.ds(q0, RT), h * LANES:(h + 1) * LANES] for g in range(gpb)]
                for p, yp in enumerate(_block_transpose(xs, masks)):
                    y_ref[pl.ds((r0 + q0) * T + p + gpb * h, RT, stride=T), :] = yp
            return carry2

        lax.fori_loop(0, RC // RT, to_tokens, 0)
        return carry

    lax.fori_loop(0, rows // RC, out_tile, 0)


def _s5_scan(u_blk, ops, h0):
    nb, nc, b, _, _ = u_blk.shape
    gpb = LANES // SSM_GROUP
    npair = gpb // 2
    W = CHUNK * SSM_GROUP
    P = SSM_STATE
    rows = nc * b
    assert rows % S5_OUT_ROWS == 0 and S5_OUT_ROWS % RELAY_ROWS == 0 and b % SUBLANES == 0
    m, p_op, q_op, a16r, a16i = ops
    pspec = pl.BlockSpec((2, npair, 1, 2 * P), lambda j: (0, j, 0, 0))
    act = lambda **kw: pl.BlockSpec((None, rows * CHUNK, LANES), lambda j: (j, 0, 0), **kw)
    st = pl.BlockSpec((None, npair, 2, 2, b, 2 * P), lambda j: (j, 0, 0, 0, 0, 0))
    in_specs = [act(pipeline_mode=pl.Buffered(1)), pl.BlockSpec((gpb, W, W), lambda j: (j, 0, 0)),
                pl.BlockSpec((gpb, W, 8 * P), lambda j: (j, 0, 0)),
                pl.BlockSpec((gpb, 8 * P, W), lambda j: (j, 0, 0)), pspec, pspec]
    args = [u_blk.reshape(nb, rows * CHUNK, LANES), m, p_op, q_op, a16r, a16i]
    if h0 is not None:
        in_specs.append(st)
        args.append(h0.reshape(b, 2, 2, nb, npair, 2, P).transpose(3, 4, 1, 2, 0, 5, 6)
                    .reshape(nb, npair, 2, 2, b, 2 * P))
    y, fin = pl.pallas_call(
        functools.partial(_s5_kernel, batch=b, has_h0=h0 is not None),
        grid=(nb,),
        in_specs=in_specs,
        out_specs=[act(), st],
        out_shape=[jax.ShapeDtypeStruct((nb, rows * CHUNK, LANES), F32),
                   jax.ShapeDtypeStruct((nb, npair, 2, 2, b, 2 * P), F32)],
        scratch_shapes=[pltpu.VMEM((gpb, rows, W), BF16),
                        pltpu.VMEM((npair, rows, 8 * P), F32),
                        pltpu.VMEM((gpb, S5_OUT_ROWS, W), F32)],
        compiler_params=_params(1),
        name="s5",
    )(*args)
    fin = (fin.reshape(nb, npair, 2, 2, b, 2, P).transpose(4, 2, 3, 0, 1, 5, 6)
           .reshape(b, 2, 2, nb * gpb, P))
    return y.reshape(u_blk.shape), fin


def _post_kernel(x_ref, attn_ref, y_ref, u_ref, d_ref, wglu_ref, bglu_ref, wout_ref, mod_ref, g2_ref,
                 wrt_ref, x1_out, h2_out, aff_out):
    m = mod_ref[...]
    gate1, shift2, scale2 = m[2:3], m[3:4], m[4:5]
    tm = x_ref.shape[0]
    tokens = lambda r: jnp.concatenate([r[gb].reshape(tm, LANES) for gb in range(r.shape[0])], axis=1)
    y = tokens(y_ref) + tokens(u_ref) * d_ref[...]
    y = jax.nn.gelu(y)
    y = y * jax.nn.sigmoid(_dot(y.astype(BF16), wglu_ref[...]) + bglu_ref[...])
    aw = attn_ref.shape[1]
    proj = _dot(attn_ref[...], wout_ref[:aw, :]) + _dot(y.astype(BF16), wout_ref[aw:, :])
    x1 = x_ref[...] + gate1 * proj
    x1_out[...] = x1
    h2 = (_rms_rows(x1, g2_ref[...]) * (1.0 + scale2) + shift2).astype(BF16)
    h2_out[...] = h2
    logits = _dot_nt(wrt_ref[...], h2)
    e = jnp.exp(logits - jnp.max(logits, axis=0, keepdims=True))
    aff_out[...] = e / jnp.sum(e, axis=0, keepdims=True)


def _post(x, attn, y_ssm, u, d_skip, wglu_bf, b_glu, wout_bf, mod, g2, wrt_bf):
    b, n, d = x.shape
    tm = min(n, TOKEN_ROWS)
    sw = u.shape[0] * LANES
    shared = mod.shape[0] == 1
    mod_idx = (lambda i, j: (0, 0, 0)) if shared else (lambda i, j: (i, 0, 0))
    const2 = lambda i, j: (0, 0)
    tok = lambda w: pl.BlockSpec((None, tm, w), lambda i, j: (i, j, 0))
    return pl.pallas_call(
        _post_kernel,
        grid=(b, n // tm),
        in_specs=[tok(d), tok(attn.shape[-1]), _chunked_spec(sw, tm), _chunked_spec(sw, tm),
                  pl.BlockSpec((1, sw), const2),
                  pl.BlockSpec(wglu_bf.shape, const2),
                  pl.BlockSpec((1, sw), const2),
                  pl.BlockSpec(wout_bf.shape, const2),
                  pl.BlockSpec((None, N_MOD, d), mod_idx),
                  pl.BlockSpec((1, d), const2),
                  pl.BlockSpec(wrt_bf.shape, const2)],
        out_specs=[tok(d), tok(d), pl.BlockSpec((None, N_EXPERTS, tm), lambda i, j: (i, 0, j))],
        out_shape=[jax.ShapeDtypeStruct((b, n, d), F32),
                   jax.ShapeDtypeStruct((b, n, d), BF16),
                   jax.ShapeDtypeStruct((b, N_EXPERTS, n), F32)],
        compiler_params=_params(2),
        name="post",
    )(x, attn, y_ssm, u, d_skip, wglu_bf, b_glu, wout_bf, mod, g2, wrt_bf)


MOE_TOKEN_TILE = 256


def _route_kernel(aff_ref, pos_out, off_out, *, cap):
    a = aff_ref[...]
    rows, n = a.shape
    capf = jnp.float32(cap)

    def count(mask):
        return jnp.sum(jnp.where(mask, 1.0, 0.0), axis=1, keepdims=True)

    def search(i, bits):
        cand = bits | (jnp.int32(1) << (30 - i))
        ok = count(a >= lax.bitcast_convert_type(cand, F32)) >= capf
        return jnp.where(ok, cand, bits)

    thr_bits = lax.fori_loop(0, 31, search, jnp.zeros((rows, 1), jnp.int32))
    thr = lax.bitcast_convert_type(thr_bits, F32)
    gt = a > thr
    eq = a == thr
    before = (lax.broadcasted_iota(jnp.int32, (n, n), 0)
              < lax.broadcasted_iota(jnp.int32, (n, n), 1))
    tri = jnp.where(before, 1.0, 0.0).astype(BF16)
    eq_rank = _dot(jnp.where(eq, 1.0, 0.0).astype(BF16), tri)
    sel = gt | (eq & (eq_rank < capf - count(gt)))
    sel_bf = jnp.where(sel, 1.0, 0.0).astype(BF16)
    pos = _dot(sel_bf, tri)
    pos_out[...] = jnp.where(sel, pos, -1.0).astype(jnp.int32)
    tile_start = lax.broadcasted_iota(jnp.int32, (n, LANES), 1) * MOE_TOKEN_TILE
    in_front = lax.broadcasted_iota(jnp.int32, (n, LANES), 0) < tile_start
    off_out[...] = _dot(sel_bf, jnp.where(in_front, 1.0, 0.0).astype(BF16)).astype(jnp.int32)


def _route(aff_t, cap):
    b, e, n = aff_t.shape
    rows = b * e
    nt = n // MOE_TOKEN_TILE
    pos, off = pl.pallas_call(
        functools.partial(_route_kernel, cap=cap),
        grid=(1,),
        in_specs=[pl.BlockSpec((rows, n), lambda i: (0, 0))],
        out_specs=[pl.BlockSpec((rows, n), lambda i: (0, 0)), pl.BlockSpec((rows, LANES), lambda i: (0, 0))],
        out_shape=[jax.ShapeDtypeStruct((rows, n), jnp.int32), jax.ShapeDtypeStruct((rows, LANES), jnp.int32)],
        compiler_params=_params(1),
        name="route",
    )(aff_t.reshape(rows, n))
    return pos.reshape(b, e, n), off[:, :nt + 1].reshape(-1)


SLOT_ALIGN = 16


def _window_rows(cap, n):
    mean = cap * MOE_TOKEN_TILE // n
    wn = min(cap, max(SLOT_ALIGN, 1 << (2 * mean - 1).bit_length()))
    assert wn & (wn - 1) == 0 and cap % SLOT_ALIGN == 0, "window rows must be a power of two"
    return wn


def _tile_windows(off_ref, request, tile, n_tiles, wn):
    starts, rounds = [], jnp.int32(0)
    for e in range(N_EXPERTS):
        i = (request * N_EXPERTS + e) * (n_tiles + 1) + tile
        start = off_ref[i] & -SLOT_ALIGN
        starts.append(start)
        rounds = jnp.maximum(rounds, lax.shift_right_logical(off_ref[i + 1] - start + (wn - 1),
                                                             jnp.int32(wn.bit_length() - 1)))
    return starts, rounds


def _window(start, r, wn, cap):
    lo = start + r * wn
    base = pl.multiple_of(jnp.minimum(lo, cap - wn), SLOT_ALIGN)
    return lo, base


def _gather_kernel(off_ref, h_ref, pos_ref, aff_ref, x_out, gate_out, p_scr, *, cap, wn):
    n = h_ref.shape[0]
    tt = MOE_TOKEN_TILE
    x_out[...] = jnp.zeros_like(x_out)
    gate_out[...] = jnp.zeros_like(gate_out)
    w_col = lax.broadcasted_iota(jnp.int32, (wn, 1), 0)
    for kt in range(n // tt):
        tile = slice(kt * tt, (kt + 1) * tt)
        starts, rounds = _tile_windows(off_ref, pl.program_id(0), kt, n // tt, wn)

        def one_round(r, carry):
            for e in range(N_EXPERTS):
                lo, base = _window(starts[e], r, wn, cap)
                sid = base + w_col
                sid = jnp.where((sid >= lo) & (sid < lo + wn), sid, -2)
                hit = pos_ref[e:e + 1, tile] == sid
                p_scr[e * wn:(e + 1) * wn, :] = jnp.where(hit, 1.0, 0.0).astype(BF16)
                gate_out[e, pl.ds(base, wn), :] += jnp.sum(jnp.where(hit, aff_ref[e:e + 1, tile], 0.0),
                                                           axis=1, keepdims=True)
            xs = _dot(p_scr[...], h_ref[tile, :]).astype(BF16)
            for e in range(N_EXPERTS):
                _, base = _window(starts[e], r, wn, cap)
                x_out[e, pl.ds(base, wn), :] += xs[e * wn:(e + 1) * wn, :]
            return carry

        lax.fori_loop(0, rounds, one_round, 0)


def _gather(h2, pos, aff_t, off, cap):
    b, n, d = h2.shape
    wn = _window_rows(cap, n)
    grid_spec = pltpu.PrefetchScalarGridSpec(
        num_scalar_prefetch=1,
        grid=(b,),
        in_specs=[pl.BlockSpec((None, n, d), lambda i, off: (i, 0, 0)),
                  pl.BlockSpec((None, N_EXPERTS, n), lambda i, off: (i, 0, 0)),
                  pl.BlockSpec((None, N_EXPERTS, n), lambda i, off: (i, 0, 0))],
        out_specs=[pl.BlockSpec((None, N_EXPERTS, cap, d), lambda i, off: (i, 0, 0, 0)),
                   pl.BlockSpec((None, N_EXPERTS, cap, 1), lambda i, off: (i, 0, 0, 0))],
        scratch_shapes=[pltpu.VMEM((N_EXPERTS * wn, MOE_TOKEN_TILE), BF16)])
    return pl.pallas_call(
        functools.partial(_gather_kernel, cap=cap, wn=wn),
        grid_spec=grid_spec,
        out_shape=[jax.ShapeDtypeStruct((b, N_EXPERTS, cap, d), BF16),
                   jax.ShapeDtypeStruct((b, N_EXPERTS, cap, 1), F32)],
        compiler_params=_params(1),
        name="gather",
    )(off, h2, pos, aff_t)


FFN_SPLIT = 4


def _ffn_kernel(x_ref, gate_ref, wg_ref, wu_ref, wd_ref, y_out):
    bt, cap, d = x_ref.shape
    fs = wg_ref.shape[1] // FFN_SPLIT
    x = x_ref[...].reshape(bt * cap, d)
    y = jnp.zeros((bt * cap, d), F32)
    for h in range(FFN_SPLIT):
        cols = slice(h * fs, (h + 1) * fs)
        g = _dot(x, wg_ref[:, cols].astype(BF16))
        hid = (g * jax.nn.sigmoid(g)) * _dot(x, wu_ref[:, cols].astype(BF16))
        y = y + _dot(hid.astype(BF16), wd_ref[cols, :].astype(BF16))
    y = y * gate_ref[...].reshape(bt * cap, 1)
    y_out[...] = y.astype(BF16).reshape(bt, cap, d)


def _ffn(xsel, gate, wg, wu, wd):
    b, e, cap, d = xsel.shape
    bt = max(1, min(b, MOE_ROWS // cap))
    f = wg.shape[-1]
    return pl.pallas_call(
        _ffn_kernel,
        grid=(e, b // bt),
        in_specs=[pl.BlockSpec((bt, None, cap, d), lambda i, j: (j, i, 0, 0)),
                  pl.BlockSpec((bt, None, cap, 1), lambda i, j: (j, i, 0, 0)),
                  pl.BlockSpec((None, d, f), lambda i, j: (i, 0, 0)),
                  pl.BlockSpec((None, d, f), lambda i, j: (i, 0, 0)),
                  pl.BlockSpec((None, f, d), lambda i, j: (i, 0, 0))],
        out_specs=pl.BlockSpec((bt, None, cap, d), lambda i, j: (j, i, 0, 0)),
        out_shape=jax.ShapeDtypeStruct((b, e, cap, d), BF16),
        compiler_params=_params(2),
        name="ffn",
    )(xsel, gate, wg, wu, wd)


def _combine_kernel(off_ref, x1_ref, y_ref, post_ref, mod_ref, gf_ref, out_ref, ycat_scr, *, cap, wn, n_tiles):
    tn, d = x1_ref.shape
    width = N_EXPERTS * wn
    post = post_ref[...].astype(F32).astype(BF16)
    lane = lax.broadcasted_iota(jnp.int32, (N_EXPERTS, width), 1)
    erow = lax.broadcasted_iota(jnp.int32, (N_EXPERTS, width), 0)
    expand = jnp.where(erow == lane // wn, 1.0, 0.0).astype(BF16)
    lane1 = lax.broadcasted_iota(jnp.int32, (1, width), 1)
    starts, rounds = _tile_windows(off_ref, pl.program_id(0), pl.program_id(1), n_tiles, wn)

    def one_round(r):
        sid = jnp.full((1, width), -2, jnp.int32)
        for e in range(N_EXPERTS):
            lo, base = _window(starts[e], r, wn, cap)
            ycat_scr[e * wn:(e + 1) * wn, :] = y_ref[e, pl.ds(base, wn), :]
            s = base + (lane1 - e * wn)
            sid = jnp.where((lane1 // wn == e) & (s >= lo) & (s < lo + wn), s, sid)
        hit = _dot(post, expand) == sid.astype(F32)
        return _dot(jnp.where(hit, 1.0, 0.0).astype(BF16), ycat_scr[...])

    acc = lax.fori_loop(1, rounds, lambda r, acc: acc + one_round(r), one_round(0))
    gate2 = mod_ref[...][5:6]
    out_ref[...] = _rms_rows(x1_ref[...] + gate2 * acc, gf_ref[...])


def _combine(x1, ysel, pos_t, off, mod, gf, cap):
    b, n, d = x1.shape
    tn = MOE_TOKEN_TILE
    wn = _window_rows(cap, n)
    shared = mod.shape[0] == 1
    mod_idx = (lambda i, j, off: (0, 0, 0)) if shared else (lambda i, j, off: (i, 0, 0))
    grid_spec = pltpu.PrefetchScalarGridSpec(
        num_scalar_prefetch=1,
        grid=(b, n // tn),
        in_specs=[pl.BlockSpec((None, tn, d), lambda i, j, off: (i, j, 0)),
                  pl.BlockSpec((None, N_EXPERTS, cap, d), lambda i, j, off: (i, 0, 0, 0)),
                  pl.BlockSpec((None, tn, N_EXPERTS), lambda i, j, off: (i, j, 0)),
                  pl.BlockSpec((None, N_MOD, d), mod_idx),
                  pl.BlockSpec((1, d), lambda i, j, off: (0, 0))],
        out_specs=pl.BlockSpec((None, tn, d), lambda i, j, off: (i, j, 0)),
        scratch_shapes=[pltpu.VMEM((N_EXPERTS * wn, d), BF16)])
    return pl.pallas_call(
        functools.partial(_combine_kernel, cap=cap, wn=wn, n_tiles=n // tn),
        grid_spec=grid_spec,
        out_shape=jax.ShapeDtypeStruct((b, n, d), F32),
        compiler_params=_params(2),
        name="combine",
    )(off, x1, ysel, pos_t, mod, gf)


def _rope_tables(n_tokens):
    pairs = HEAD_DIM // 4
    rows = n_tokens // GRID_W
    row = jnp.repeat(jnp.arange(rows, dtype=F32), GRID_W)
    col = jnp.tile(jnp.arange(GRID_W, dtype=F32), rows)
    inv_freq = ROPE_THETA ** (-jnp.arange(pairs, dtype=F32) / pairs)
    ang = jnp.concatenate([row[:, None] * inv_freq, col[:, None] * inv_freq], axis=-1)
    reps = LANES // (HEAD_DIM // 2)
    return jnp.tile(jnp.cos(ang), (1, reps)), jnp.tile(jnp.sin(ang), (1, reps))


def _trunk(x, mod, w, rope_tabs, ctx_k, ctx_v, ctx_state):
    b, n, d = x.shape
    q, k, v, u = _inproj(x, mod, w["g1"], w["w_in"], w["qg"], w["kg"], w["bd"], rope_tabs)
    kv = [(k, v)]
    if ctx_k is not None:
        kv.append((ctx_k.reshape(b, -1, KV_WIDTH), ctx_v.reshape(b, -1, KV_WIDTH)))
    attn = _attention(q, kv)
    y_ssm, state = _s5_scan(u, w["s5ops"], ctx_state)
    x1, h2, aff_t = _post(x, attn, y_ssm, u, w["d_skip"], w["w_glu"], w["b_glu"], w["w_out"], mod,
                          w["g2"], w["w_router_t"])
    cap = CAPACITY_FACTOR * n // N_EXPERTS
    pos, off = _route(aff_t, cap)
    xsel, gate = _gather(h2, pos, aff_t, off, cap)
    ysel = _ffn(xsel, gate, w["w_gate"], w["w_up"], w["w_down"])
    out = _combine(x1, ysel, pos.transpose(0, 2, 1), off, mod, w["gf"], cap)
    return out, k, v, state


def kernel(x_prompt, x_sample, cache_k, cache_v, state_ssm, c, c_ctx, norm1_g, norm2_g, w_mod, b_mod, w_in, q_norm_g, k_norm_g, ssm_a_re, ssm_a_im, ssm_log_dt, ssm_b_re, ssm_b_im, ssm_c_re, ssm_c_im, ssm_d, w_glu, b_glu, w_out, w_router, w_gate, w_up, w_down, final_norm_g):
    assert norm1_g.shape[0] == 1, "single trunk layer"
    d = x_prompt.shape[-1]
    n_dec = c.shape[0]
    n_cond = -(-(1 + n_dec) // SLOT_ALIGN) * SLOT_ALIGN
    cond = jnp.concatenate([c_ctx[None, :], c, jnp.zeros((n_cond - 1 - n_dec, d), F32)], axis=0)
    mod = _modulation(cond, w_mod[0], b_mod[0]).reshape(n_cond, N_MOD, d)
    head = jnp.arange(REP * HEAD_DIM) // HEAD_DIM
    w = {
        "g1": norm1_g, "g2": norm2_g, "gf": final_norm_g.reshape(1, d),
        "w_in": w_in[0].astype(BF16),
        "qg": jnp.tile(q_norm_g, (1, REP)),
        "kg": jnp.tile(k_norm_g, (1, REP)),
        "bd": (head[:, None] == head[None, :]).astype(BF16),
        "s5ops": _s5_prep(ssm_a_re[0], ssm_a_im[0], ssm_log_dt[0], ssm_b_re[0], ssm_b_im[0],
                          ssm_c_re[0], ssm_c_im[0]),
        "d_skip": ssm_d, "w_glu": w_glu[0].astype(BF16), "b_glu": b_glu,
        "w_out": w_out[0].astype(BF16),
        "w_router_t": w_router[0].T.astype(BF16),
        "w_gate": w_gate[0], "w_up": w_up[0], "w_down": w_down[0],
    }
    y_p, k_ctx, v_ctx, st_ctx = _trunk(x_prompt, mod[0:1], w, None, None, None, None)
    y_s, _, _, _ = _trunk(x_sample, mod[1:1 + n_dec], w, _rope_tables(x_sample.shape[1]),
                          cache_k[:, 0], cache_v[:, 0], state_ssm[:, 0])
    bp, np_ = x_prompt.shape[:2]
    new_k = k_ctx.reshape(bp, 1, np_, N_KV_HEADS, HEAD_DIM)
    new_v = v_ctx.reshape(bp, 1, np_, N_KV_HEADS, HEAD_DIM)
    return (y_p, y_s, new_k, new_v, st_ctx[:, None])
```
